```python
import jax, jax.numpy as jnp
from jax import lax
import numpy as np

D_MODEL = 4096
BATCH = 2
SEQ = 4096
DEPTH = 2

N_META = 16
MIX_W = D_MODEL
RET_W = MIX_W // 2
POOL_W = MIX_W - RET_W
RET_HEADS = 8
RET_HEAD_DIM = RET_W // RET_HEADS
CHUNK = 128
META_PAD = (-N_META) % CHUNK
POOL_WINDOWS = (2, 4, 8, 16)
N_POOL_GROUPS = len(POOL_WINDOWS)
POOL_GROUP = POOL_W // N_POOL_GROUPS
IN_COLS = 4 * RET_W + POOL_W
D_FF = ((8 * D_MODEL // 3 + 255) // 256) * 256
CONV_W = 3
ROPE_BASE = 10000.0
EPS = 1e-6

kernel_name = "hymba_retention_pool_convglu"


def rms_norm(x, g):
    xf = x.astype(jnp.float32)
    y = xf * lax.rsqrt(jnp.mean(xf * xf, axis=-1, keepdims=True) + EPS)
    return (y * g.astype(jnp.float32)).astype(x.dtype)


def rotary(x, pos):
    half = x.shape[-1] // 2
    inv = jnp.power(ROPE_BASE, -jnp.arange(half, dtype=jnp.float32) / half)
    ang = pos.astype(jnp.float32)[:, None] * inv[None, :]
    cos = jnp.cos(ang)[None, :, None, :]
    sin = jnp.sin(ang)[None, :, None, :]
    xf = x.astype(jnp.float32)
    x1, x2 = xf[..., :half], xf[..., half:]
    return jnp.concatenate([x1 * cos - x2 * sin, x1 * sin + x2 * cos], axis=-1).astype(x.dtype)


def retention(q, k, v):
    B, T, H, Dh = q.shape
    N = T // CHUNK
    f32 = jnp.float32
    log_gamma = jnp.log1p(-jnp.exp2(-5.0 - jnp.arange(H, dtype=f32)))
    idx = jnp.arange(CHUNK, dtype=f32)
    rel = idx[:, None] - idx[None, :]
    intra_decay = jnp.where(rel[None] >= 0,
                            jnp.exp(log_gamma[:, None, None] * jnp.maximum(rel, 0.0)[None]), 0.0)
    q_decay = jnp.exp(log_gamma[None, :] * (idx + 1.0)[:, None])
    k_decay = jnp.exp(log_gamma[None, :] * (CHUNK - 1.0 - idx)[:, None])
    chunk_decay = jnp.exp(log_gamma * CHUNK)

    qc = q.astype(f32).reshape(B, N, CHUNK, H, Dh)
    kc = k.astype(f32).reshape(B, N, CHUNK, H, Dh)
    vc = v.astype(f32).reshape(B, N, CHUNK, H, Dh)

    scores = jnp.einsum('bnchd,bnshd->bnhcs', qc, kc) * intra_decay[None, None]
    intra = jnp.einsum('bnhcs,bnshe->bnche', scores, vc)

    def step(state, xs):
        qn, kn, vn = xs
        cross = jnp.einsum('bchd,bhde->bche', qn * q_decay[None, :, :, None], state)
        state = state * chunk_decay[None, :, None, None] + jnp.einsum(
            'bchd,bche->bhde', kn * k_decay[None, :, :, None], vn)
        return state, cross

    state0 = jnp.zeros((B, H, Dh, Dh), f32)
    _, cross = lax.scan(step, state0, (jnp.moveaxis(qc, 1, 0), jnp.moveaxis(kc, 1, 0), jnp.moveaxis(vc, 1, 0)))
    cross = jnp.moveaxis(cross, 0, 1)
    return (intra + cross).reshape(B, T, H, Dh)


def multi_scale_pool(p, pool_w, pool_scale):
    B, L, _ = p.shape
    t = jnp.arange(L)
    outs = []
    for gi, w in enumerate(POOL_WINDOWS):
        xg = p[..., gi * POOL_GROUP:(gi + 1) * POOL_GROUP].astype(jnp.float32)
        cs = jnp.cumsum(xg, axis=1)
        cs_lag = jnp.pad(cs, ((0, 0), (w, 0), (0, 0)))[:, :L]
        cnt = jnp.minimum(t + 1, w).astype(jnp.float32)[None, :, None]
        mix = ((cs - cs_lag) / cnt - xg).astype(p.dtype)
        outs.append(jnp.einsum('blc,cd->bld', mix, pool_w[gi]))
    return jnp.concatenate(outs, axis=-1) * pool_scale


def hybrid_mixer(hn, w_in, pool_w, pool_scale, w_out, pos):
    B, L, _ = hn.shape
    proj = hn @ w_in
    q = proj[..., 0:RET_W].reshape(B, L, RET_HEADS, RET_HEAD_DIM)
    k = proj[..., RET_W:2 * RET_W].reshape(B, L, RET_HEADS, RET_HEAD_DIM)
    v = proj[..., 2 * RET_W:3 * RET_W].reshape(B, L, RET_HEADS, RET_HEAD_DIM)
    g = proj[..., 3 * RET_W:4 * RET_W]
    p = proj[..., 4 * RET_W:]

    q = rotary(q, pos)
    k = rotary(k, pos) * (RET_HEAD_DIM ** -0.5)
    pad = ((0, 0), (META_PAD, 0), (0, 0), (0, 0))
    r = retention(jnp.pad(q, pad), jnp.pad(k, pad), jnp.pad(v, pad))[:, META_PAD:]
    r = r * lax.rsqrt(jnp.mean(r * r, axis=-1, keepdims=True) + EPS)
    r = (r.reshape(B, L, RET_W) * jax.nn.silu(g.astype(jnp.float32))).astype(hn.dtype)

    m = multi_scale_pool(p, pool_w, pool_scale)

    return jnp.concatenate([r, m], axis=-1) @ w_out


def conv_glu_ffn(x, w_up, conv_w, conv_b, w_down):
    L = x.shape[1]
    u = x @ w_up
    a, b = u[..., :D_FF], u[..., D_FF:]
    ap = jnp.pad(a, ((0, 0), (CONV_W - 1, 0), (0, 0)))
    ac = conv_b
    for tap in range(CONV_W):
        ac = ac + ap[:, tap:tap + L] * conv_w[tap]
    return (jax.nn.silu(ac) * b) @ w_down


def setup_inputs(seed: int = 0) -> dict:
    key = jax.random.key(seed)
    ks = jax.random.split(key, 13)
    f32 = jnp.float32

    def nrm(k, shape, scale):
        return jax.random.normal(k, shape, f32) * scale

    return {
        "x": nrm(ks[0], (BATCH, SEQ, D_MODEL), 1.0),
        "meta_tokens": nrm(ks[1], (N_META, D_MODEL), 1.0),
        "norm1_g": 1.0 + nrm(ks[2], (DEPTH, D_MODEL), 0.02),
        "w_in": nrm(ks[3], (DEPTH, D_MODEL, IN_COLS), D_MODEL ** -0.5),
        "pool_w": nrm(ks[4], (DEPTH, N_POOL_GROUPS, POOL_GROUP, POOL_GROUP), POOL_GROUP ** -0.5),
        "pool_scale": 1.0 + nrm(ks[5], (DEPTH, POOL_W), 0.02),
        "w_out": nrm(ks[6], (DEPTH, MIX_W, D_MODEL), MIX_W ** -0.5),
        "norm2_g": 1.0 + nrm(ks[7], (DEPTH, D_MODEL), 0.02),
        "w_up": nrm(ks[8], (DEPTH, D_MODEL, 2 * D_FF), D_MODEL ** -0.5),
        "conv_w": nrm(ks[9], (DEPTH, CONV_W, D_FF), CONV_W ** -0.5),
        "conv_b": nrm(ks[10], (DEPTH, D_FF), 0.01),
        "w_down": nrm(ks[11], (DEPTH, D_FF, D_MODEL), D_FF ** -0.5),
        "final_g": 1.0 + nrm(ks[12], (D_MODEL,), 0.02),
    }


def reference(x, meta_tokens, norm1_g, w_in, pool_w, pool_scale, w_out, norm2_g, w_up, conv_w, conv_b, w_down, final_g):
    B = x.shape[0]
    meta = jnp.broadcast_to(meta_tokens[None].astype(x.dtype), (B, N_META, D_MODEL))
    h = jnp.concatenate([meta, x], axis=1)
    pos = jnp.arange(h.shape[1])
    for l in range(DEPTH):
        h = h + hybrid_mixer(rms_norm(h, norm1_g[l]), w_in[l], pool_w[l], pool_scale[l], w_out[l], pos)
        h = h + conv_glu_ffn(rms_norm(h, norm2_g[l]), w_up[l], conv_w[l], conv_b[l], w_down[l])
    return rms_norm(h, final_g)[:, N_META:]
```

```python
import functools
import math

import jax
import jax.numpy as jnp
from jax import lax
from jax.experimental import pallas as pl
from jax.experimental.pallas import tpu as pltpu

D_MODEL = 4096
N_META = 16
RET_W = 2048
POOL_W = 2048
RET_HEADS = 8
HEAD_DIM = 256
HALF = HEAD_DIM // 2
POOL_WINDOWS = (2, 4, 8, 16)
POOL_GROUP = 512
D_FF = 11008
CONV_W = 3
ROPE_BASE = 10000.0
EPS = 1e-6

LANE = 128
ROW_ALIGN = 128
META_PAD = (-N_META) % ROW_ALIGN
D_FF_PAD = 11264
POOL_HALO = 16
CONV_HALO = 8
VMEM_LIMIT = 58 * 1024 * 1024

BF16 = jnp.bfloat16
F32 = jnp.float32


def _params(sem):
    return pltpu.CompilerParams(dimension_semantics=sem, vmem_limit_bytes=VMEM_LIMIT)


def _rms_rows(x, g):
    return x * lax.rsqrt(jnp.mean(x * x, axis=-1, keepdims=True) + EPS) * g


def _in_proj_kernel(h_ref, g_ref, w_ref, cos_ref, sin_ref, qkv_ref, gp_ref, hn_ref, *, tn, n_rot, n_qkv):
    j = pl.program_id(1)

    @pl.when(j == 0)
    def _():
        hn_ref[...] = _rms_rows(h_ref[...], g_ref[...]).astype(BF16)

    acc = jnp.dot(hn_ref[...], w_ref[...], preferred_element_type=F32)

    @pl.when(j < n_rot)
    def _():
        cos = cos_ref[...]
        sin = sin_ref[...]
        for hh in range(tn // HEAD_DIM):
            x1 = acc[:, hh * HEAD_DIM: hh * HEAD_DIM + HALF]
            x2 = acc[:, hh * HEAD_DIM + HALF: (hh + 1) * HEAD_DIM]
            qkv_ref[:, hh * HEAD_DIM: hh * HEAD_DIM + HALF] = (x1 * cos - x2 * sin).astype(BF16)
            qkv_ref[:, hh * HEAD_DIM + HALF: (hh + 1) * HEAD_DIM] = (x1 * sin + x2 * cos).astype(BF16)

    @pl.when(jnp.logical_and(j >= n_rot, j < n_qkv))
    def _():
        qkv_ref[...] = acc.astype(BF16)

    @pl.when(j >= n_qkv)
    def _():
        gp_ref[...] = acc


def _in_proj(h, g, w, cos, sin, *, tm, tn):
    m = h.shape[0]
    n = w.shape[1]
    n_rot = 2 * RET_W // tn
    n_qkv = 3 * RET_W // tn
    n_j = n // tn
    kern = functools.partial(_in_proj_kernel, tn=tn, n_rot=n_rot, n_qkv=n_qkv)
    return pl.pallas_call(
        kern,
        grid=(m // tm, n_j),
        in_specs=[
            pl.BlockSpec((tm, D_MODEL), lambda i, j: (i, 0), pipeline_mode=pl.Buffered(1)),
            pl.BlockSpec((1, D_MODEL), lambda i, j: (0, 0)),
            pl.BlockSpec((D_MODEL, tn), lambda i, j: (0, j)),
            pl.BlockSpec((tm, HALF), lambda i, j: (i, 0)),
            pl.BlockSpec((tm, HALF), lambda i, j: (i, 0)),
        ],
        out_specs=[
            pl.BlockSpec((tm, tn), lambda i, j: (i, jnp.minimum(j, n_qkv - 1))),
            pl.BlockSpec((tm, tn), lambda i, j: (i, jnp.maximum(j - n_qkv, 0))),
        ],
        out_shape=[
            jax.ShapeDtypeStruct((m, 3 * RET_W), BF16),
            jax.ShapeDtypeStruct((m, RET_W + POOL_W), F32),
        ],
        scratch_shapes=[pltpu.VMEM((tm, D_MODEL), BF16)],
        compiler_params=_params(("arbitrary", "arbitrary")),
        name="in_proj",
    )(h, g, w, cos, sin)


def _mixer_kernel(q_ref, k_ref, v_ref, g_ref, p_ref, dmask_ref, qd_ref, kd_ref, cd_ref, pw_ref, ps_ref,
                  out_ref, state_ref, pext_ref, *, chunk):
    c = pl.program_id(1)

    @pl.when(c == 0)
    def _():
        state_ref[...] = jnp.zeros_like(state_ref)
        pext_ref[0:POOL_HALO, :] = jnp.zeros((POOL_HALO, POOL_W), F32)

    for hh in range(RET_HEADS):
        sl = slice(hh * HEAD_DIM, (hh + 1) * HEAD_DIM)
        q = q_ref[:, sl]
        k = k_ref[:, sl]
        v = v_ref[:, sl]
        scores = lax.dot_general(q, k, (((1,), (1,)), ((), ())), preferred_element_type=F32)
        scores = (scores * dmask_ref[hh]).astype(BF16)
        intra = jnp.dot(scores, v, preferred_element_type=F32)
        state = state_ref[hh]
        cross = jnp.dot(q, state.astype(BF16), preferred_element_type=F32) * qd_ref[hh]
        kdec = (k.astype(F32) * kd_ref[hh]).astype(BF16)
        upd = lax.dot_general(kdec, v, (((0,), (0,)), ((), ())), preferred_element_type=F32)
        state_ref[hh] = state * cd_ref[hh] + upd
        r = intra + cross
        r = r * lax.rsqrt(jnp.mean(r * r, axis=-1, keepdims=True) + EPS)
        gate = g_ref[:, sl]
        out_ref[:, sl] = (r * (gate * jax.nn.sigmoid(gate))).astype(BF16)

    pext_ref[POOL_HALO:POOL_HALO + chunk, :] = p_ref[...]
    t = c * chunk - META_PAD + lax.broadcasted_iota(jnp.int32, (chunk, 1), 0)
    valid = t >= 0
    for gi, w in enumerate(POOL_WINDOWS):
        cs = slice(gi * POOL_GROUP, (gi + 1) * POOL_GROUP)
        xg = pext_ref[POOL_HALO:POOL_HALO + chunk, cs]
        win = xg
        for d in range(1, w):
            win = win + pext_ref[POOL_HALO - d:POOL_HALO - d + chunk, cs]
        cnt = jnp.clip(t + 1, 1, w).astype(F32)
        mix = jnp.where(valid, win / cnt - xg, 0.0).astype(BF16)
        mg = jnp.dot(mix, pw_ref[gi], preferred_element_type=F32) * ps_ref[:, cs]
        out_ref[:, RET_W + gi * POOL_GROUP: RET_W + (gi + 1) * POOL_GROUP] = mg.astype(BF16)
    pext_ref[0:POOL_HALO, :] = pext_ref[chunk:chunk + POOL_HALO, :]


def _mixer(qkv, gp, dmask, qd, kd, cd, pool_w, pool_scale, *, batch, lp, chunk):
    nc = lp // chunk
    m = qkv.shape[0]
    row = lambda b, c: b * nc + c
    kern = functools.partial(_mixer_kernel, chunk=chunk)
    const3 = lambda b, c: (0, 0, 0)
    return pl.pallas_call(
        kern,
        grid=(batch, nc),
        in_specs=[
            pl.BlockSpec((chunk, RET_W), lambda b, c: (row(b, c), 0)),
            pl.BlockSpec((chunk, RET_W), lambda b, c: (row(b, c), 1)),
            pl.BlockSpec((chunk, RET_W), lambda b, c: (row(b, c), 2)),
            pl.BlockSpec((chunk, RET_W), lambda b, c: (row(b, c), 0)),
            pl.BlockSpec((chunk, POOL_W), lambda b, c: (row(b, c), 1)),
            pl.BlockSpec((RET_HEADS, chunk, chunk), const3),
            pl.BlockSpec((RET_HEADS, chunk, 1), const3),
            pl.BlockSpec((RET_HEADS, chunk, 1), const3),
            pl.BlockSpec((RET_HEADS, 1, 1), const3),
            pl.BlockSpec((len(POOL_WINDOWS), POOL_GROUP, POOL_GROUP), const3),
            pl.BlockSpec((1, POOL_W), lambda b, c: (0, 0)),
        ],
        out_specs=pl.BlockSpec((chunk, RET_W + POOL_W), lambda b, c: (row(b, c), 0)),
        out_shape=jax.ShapeDtypeStruct((m, RET_W + POOL_W), BF16),
        scratch_shapes=[
            pltpu.VMEM((RET_HEADS, HEAD_DIM, HEAD_DIM), F32),
            pltpu.VMEM((POOL_HALO + chunk, POOL_W), F32),
        ],
        compiler_params=_params(("arbitrary", "arbitrary")),
        name="mixer",
    )(qkv, qkv, qkv, gp, gp, dmask, qd, kd, cd, pool_w, pool_scale)


def _proj_res_kernel(lhs_ref, w_ref, res_ref, out_ref):
    out_ref[...] = res_ref[...] + jnp.dot(lhs_ref[...], w_ref[...], preferred_element_type=F32)


def _proj_res(lhs, w, res, *, tm, tn, name):
    m, k = lhs.shape
    n = w.shape[1]
    return pl.pallas_call(
        _proj_res_kernel,
        grid=(m // tm, n // tn),
        in_specs=[
            pl.BlockSpec((tm, k), lambda i, j: (i, 0), pipeline_mode=pl.Buffered(1)),
            pl.BlockSpec((k, tn), lambda i, j: (0, j)),
            pl.BlockSpec((tm, tn), lambda i, j: (i, j)),
        ],
        out_specs=pl.BlockSpec((tm, tn), lambda i, j: (i, j)),
        out_shape=jax.ShapeDtypeStruct((m, n), F32),
        compiler_params=_params(("arbitrary", "arbitrary")),
        name=name,
    )(lhs, w, res)


def _ffn_up_kernel(h_ref, g_ref, wa_ref, wb_ref, cw_ref, cb_ref, act_ref, hn_ref, aext_ref, carry_ref, *, tm):
    i = pl.program_id(0)
    j = pl.program_id(1)

    @pl.when(j == 0)
    def _():
        hn_ref[...] = _rms_rows(h_ref[...], g_ref[...]).astype(BF16)

    @pl.when(i == 0)
    def _():
        carry_ref[j] = jnp.zeros(carry_ref.shape[1:], F32)

    hn = hn_ref[...]
    a = jnp.dot(hn, wa_ref[...], preferred_element_type=F32)
    b = jnp.dot(hn, wb_ref[...], preferred_element_type=F32)
    aext_ref[0:CONV_HALO, :] = carry_ref[j]
    aext_ref[CONV_HALO:CONV_HALO + tm, :] = a
    carry_ref[j] = a[tm - CONV_HALO:tm, :]
    ac = cb_ref[...] + a * cw_ref[CONV_W - 1:CONV_W, :]
    for tap in range(CONV_W - 1):
        back = CONV_W - 1 - tap
        ac = ac + aext_ref[CONV_HALO - back:CONV_HALO - back + tm, :] * cw_ref[tap:tap + 1, :]
    act_ref[...] = (ac * jax.nn.sigmoid(ac) * b).astype(BF16)


def _ffn_up(h, g, wa, wb, cw, cb, *, tm, tn):
    m = h.shape[0]
    n = wa.shape[1]
    n_j = n // tn
    kern = functools.partial(_ffn_up_kernel, tm=tm)
    return pl.pallas_call(
        kern,
        grid=(m // tm, n_j),
        in_specs=[
            pl.BlockSpec((tm, D_MODEL), lambda i, j: (i, 0), pipeline_mode=pl.Buffered(1)),
            pl.BlockSpec((1, D_MODEL), lambda i, j: (0, 0)),
            pl.BlockSpec((D_MODEL, tn), lambda i, j: (0, j)),
            pl.BlockSpec((D_MODEL, tn), lambda i, j: (0, j)),
            pl.BlockSpec((CONV_W, tn), lambda i, j: (0, j)),
            pl.BlockSpec((1, tn), lambda i, j: (0, j)),
        ],
        out_specs=pl.BlockSpec((tm, tn), lambda i, j: (i, j)),
        out_shape=jax.ShapeDtypeStruct((m, n), BF16),
        scratch_shapes=[
            pltpu.VMEM((tm, D_MODEL), BF16),
            pltpu.VMEM((CONV_HALO + tm, tn), F32),
            pltpu.VMEM((n_j, CONV_HALO, tn), F32),
        ],
        compiler_params=_params(("arbitrary", "arbitrary")),
        name="ffn_up",
    )(h, g, wa, wb, cw, cb)


def _final_norm_kernel(h_ref, g_ref, out_ref):
    out_ref[0] = _rms_rows(h_ref[...], g_ref[...])


def _final_norm(h, g, *, batch, seq, lp, tr):
    skip = (META_PAD + N_META) // tr
    per_b = lp // tr
    return pl.pallas_call(
        _final_norm_kernel,
        grid=(batch, seq // tr),
        in_specs=[
            pl.BlockSpec((tr, D_MODEL), lambda b, c: (b * per_b + skip + c, 0)),
            pl.BlockSpec((1, D_MODEL), lambda b, c: (0, 0)),
        ],
        out_specs=pl.BlockSpec((1, tr, D_MODEL), lambda b, c: (b, c, 0)),
        out_shape=jax.ShapeDtypeStruct((batch, seq, D_MODEL), F32),
        compiler_params=_params(("arbitrary", "arbitrary")),
        name="final_norm",
    )(h, g)


def _decay_tables(chunk):
    log_gamma = jnp.log1p(-jnp.exp2(-5.0 - jnp.arange(RET_HEADS, dtype=F32)))
    idx = jnp.arange(chunk, dtype=F32)
    rel = idx[:, None] - idx[None, :]
    dmask = jnp.where(rel[None] >= 0, jnp.exp(log_gamma[:, None, None] * jnp.maximum(rel, 0.0)[None]), 0.0)
    qd = jnp.exp(log_gamma[:, None] * (idx + 1.0)[None, :])[:, :, None]
    kd = jnp.exp(log_gamma[:, None] * (chunk - 1.0 - idx)[None, :])[:, :, None]
    cd = jnp.exp(log_gamma * chunk)[:, None, None]
    return dmask, qd, kd, cd


def kernel(x, meta_tokens, norm1_g, w_in, pool_w, pool_scale, w_out, norm2_g, w_up, conv_w, conv_b, w_down, final_g):
    batch, seq, d = x.shape
    depth = w_in.shape[0]
    assert d == D_MODEL and meta_tokens.shape == (N_META, D_MODEL)
    lp = META_PAD + N_META + seq
    m = batch * lp
    tm = m // 8
    tm_norm = m // 12
    chunk = 384
    assert lp % chunk == 0 and m % tm == 0 and tm % 16 == 0 and m % tm_norm == 0 and tm_norm % 16 == 0

    meta = jnp.broadcast_to(meta_tokens[None].astype(x.dtype), (batch, N_META, d))
    h = jnp.concatenate([jnp.zeros((batch, META_PAD, d), x.dtype), meta, x], axis=1).reshape(m, d)

    pos = (jnp.arange(lp) - META_PAD).astype(F32)
    inv = jnp.power(ROPE_BASE, -jnp.arange(HALF, dtype=F32) / HALF)
    ang = pos[:, None] * inv[None, :]
    cos = jnp.tile(jnp.cos(ang), (batch, 1))
    sin = jnp.tile(jnp.sin(ang), (batch, 1))
    dmask, qd, kd, cd = _decay_tables(chunk)

    ff_pad = D_FF_PAD - D_FF
    for l in range(depth):
        col_scale = jnp.ones((w_in.shape[2],), F32).at[RET_W:2 * RET_W].set(HEAD_DIM ** -0.5)
        w_in_l = (w_in[l] * col_scale).astype(BF16)
        qkv, gp = _in_proj(h, norm1_g[l][None], w_in_l, cos, sin, tm=tm_norm, tn=512)
        mixed = _mixer(qkv, gp, dmask, qd, kd, cd, pool_w[l].astype(BF16), pool_scale[l][None],
                       batch=batch, lp=lp, chunk=chunk)
        h = _proj_res(mixed, w_out[l].astype(BF16), h, tm=tm, tn=1024, name="out_proj")

        wa = jnp.pad(w_up[l][:, :D_FF], ((0, 0), (0, ff_pad))).astype(BF16)
        wb = jnp.pad(w_up[l][:, D_FF:], ((0, 0), (0, ff_pad))).astype(BF16)
        cw = jnp.pad(conv_w[l], ((0, 0), (0, ff_pad)))
        cb = jnp.pad(conv_b[l], ((0, ff_pad),))[None]
        act = _ffn_up(h, norm2_g[l][None], wa, wb, cw, cb, tm=tm_norm, tn=512)
        wd = jnp.pad(w_down[l], ((0, ff_pad), (0, 0))).astype(BF16)
        h = _proj_res(act, wd, h, tm=tm, tn=256, name="ffn_down")

    return _final_norm(h, final_g[None], batch=batch, seq=seq, lp=lp, tr=ROW_ALIGN)
```

```python
import functools

import jax
import jax.numpy as jnp
from jax import lax
from jax.experimental import pallas as pl
from jax.experimental.pallas import tpu as pltpu

D_MODEL = 4096
N_META = 16
RET_W = 2048
POOL_W = 2048
RET_HEADS = 8
HEAD_DIM = 256
HALF = HEAD_DIM // 2
POOL_WINDOWS = (2, 4, 8, 16)
POOL_GROUP = 512
D_FF = 11008
CONV_W = 3
ROPE_BASE = 10000.0
EPS = 1e-6

ROW_ALIGN = 128
META_PAD = (-N_META) % ROW_ALIGN
POOL_HALO = 16
CONV_HALO = 8
NORM_CHUNKS = 8
ROW_PARTS = 4
VMEM_LIMIT = 58 * 1024 * 1024

BF16 = jnp.bfloat16
F32 = jnp.float32


def _params(sem):
    return pltpu.CompilerParams(dimension_semantics=sem, vmem_limit_bytes=VMEM_LIMIT)


def _rms_rows(x, g):
    return x * lax.rsqrt(jnp.mean(x * x, axis=-1, keepdims=True) + EPS) * g


def _norm_phase(jj, h_ref, g_ref, hn_ref, rc):
    @pl.when(jj < NORM_CHUNKS)
    def _():
        r0 = pl.multiple_of(jj * rc, 16)
        hn_ref[pl.ds(r0, rc), :] = _rms_rows(h_ref[...], g_ref[...]).astype(BF16)


def _norm_specs(tm, rc):
    h_spec = pl.BlockSpec((rc, D_MODEL), lambda i, jj: (i * NORM_CHUNKS + jnp.minimum(jj, NORM_CHUNKS - 1), 0))
    g_spec = pl.BlockSpec((1, D_MODEL), lambda i, jj: (0, 0))
    return h_spec, g_spec


def _col(jj):
    return jnp.maximum(jj - NORM_CHUNKS, 0)


def _in_proj_kernel(h_ref, g_ref, w_ref, cos_ref, sin_ref, out_ref, hn_ref, *, rc, tn, n_q, n_rot):
    jj = pl.program_id(1)
    _norm_phase(jj, h_ref, g_ref, hn_ref, rc)

    @pl.when(jj >= NORM_CHUNKS)
    def _():
        j = jj - NORM_CHUNKS
        w = w_ref[...].astype(BF16)
        is_rot = j < n_rot
        scale = jnp.where(jnp.logical_and(j >= n_q, is_rot), HEAD_DIM ** -0.5, 1.0).astype(F32)
        tp = hn_ref.shape[0] // ROW_PARTS
        for p in range(ROW_PARTS):
            rows = slice(p * tp, (p + 1) * tp)
            acc = jnp.dot(hn_ref[rows, :], w, preferred_element_type=F32)
            cos = jnp.where(is_rot, cos_ref[rows, :], 1.0) * scale
            sin = jnp.where(is_rot, sin_ref[rows, :], 0.0) * scale
            for hh in range(tn // HEAD_DIM):
                x1 = acc[:, hh * HEAD_DIM: hh * HEAD_DIM + HALF]
                x2 = acc[:, hh * HEAD_DIM + HALF: (hh + 1) * HEAD_DIM]
                out_ref[rows, hh * HEAD_DIM: hh * HEAD_DIM + HALF] = (x1 * cos - x2 * sin).astype(BF16)
                out_ref[rows, hh * HEAD_DIM + HALF: (hh + 1) * HEAD_DIM] = (x1 * sin + x2 * cos).astype(BF16)


def _in_proj(h, g, w, cos, sin, *, layer, tm, tn):
    m = h.shape[0]
    n = w.shape[2]
    rc = tm // NORM_CHUNKS
    kern = functools.partial(_in_proj_kernel, rc=rc, tn=tn, n_q=RET_W // tn, n_rot=2 * RET_W // tn)
    h_spec, g_spec = _norm_specs(tm, rc)
    return pl.pallas_call(
        kern,
        grid=(m // tm, NORM_CHUNKS + n // tn),
        in_specs=[
            h_spec,
            g_spec,
            pl.BlockSpec((None, D_MODEL, tn), lambda i, jj: (layer, 0, _col(jj))),
            pl.BlockSpec((tm, HALF), lambda i, jj: (i, 0)),
            pl.BlockSpec((tm, HALF), lambda i, jj: (i, 0)),
        ],
        out_specs=pl.BlockSpec((tm, tn), lambda i, jj: (i, _col(jj))),
        out_shape=jax.ShapeDtypeStruct((m, n), BF16),
        scratch_shapes=[pltpu.VMEM((tm, D_MODEL), BF16)],
        compiler_params=_params(("arbitrary", "arbitrary")),
        name="in_proj",
    )(h, g, w, cos, sin)


def _mixer_kernel(q_ref, k_ref, v_ref, g_ref, p_ref, dmask_ref, qd_ref, kd_ref, cd_ref, pw_ref, ps_ref,
                  out_ref, state_ref, pext_ref, *, chunk):
    c = pl.program_id(1)

    @pl.when(c == 0)
    def _():
        state_ref[...] = jnp.zeros_like(state_ref)
        pext_ref[0:POOL_HALO, :] = jnp.zeros((POOL_HALO, POOL_W), F32)

    for hh in range(RET_HEADS):
        sl = slice(hh * HEAD_DIM, (hh + 1) * HEAD_DIM)
        q = q_ref[:, sl]
        k = k_ref[:, sl]
        v = v_ref[:, sl]
        scores = lax.dot_general(q, k, (((1,), (1,)), ((), ())), preferred_element_type=F32)
        scores = (scores * dmask_ref[hh]).astype(BF16)
        intra = jnp.dot(scores, v, preferred_element_type=F32)
        state = state_ref[hh]
        cross = jnp.dot(q, state.astype(BF16), preferred_element_type=F32) * qd_ref[hh]
        kdec = (k.astype(F32) * kd_ref[hh]).astype(BF16)
        upd = lax.dot_general(kdec, v, (((0,), (0,)), ((), ())), preferred_element_type=F32)
        state_ref[hh] = state * cd_ref[hh] + upd
        r = intra + cross
        r = r * lax.rsqrt(jnp.mean(r * r, axis=-1, keepdims=True) + EPS)
        gate = g_ref[:, sl].astype(F32)
        out_ref[:, sl] = (r * (gate * jax.nn.sigmoid(gate))).astype(BF16)

    pext_ref[POOL_HALO:POOL_HALO + chunk, :] = p_ref[...].astype(F32)
    t = c * chunk - META_PAD + lax.broadcasted_iota(jnp.int32, (chunk, 1), 0)
    valid = t >= 0
    for gi, w in enumerate(POOL_WINDOWS):
        cs = slice(gi * POOL_GROUP, (gi + 1) * POOL_GROUP)
        xg = pext_ref[POOL_HALO:POOL_HALO + chunk, cs]
        win = xg
        for d in range(1, w):
            win = win + pext_ref[POOL_HALO - d:POOL_HALO - d + chunk, cs]
        cnt = jnp.clip(t + 1, 1, w).astype(F32)
        mix = jnp.where(valid, win / cnt - xg, 0.0).astype(BF16)
        mg = jnp.dot(mix, pw_ref[gi], preferred_element_type=F32) * ps_ref[:, cs]
        out_ref[:, RET_W + gi * POOL_GROUP: RET_W + (gi + 1) * POOL_GROUP] = mg.astype(BF16)
    pext_ref[0:POOL_HALO, :] = pext_ref[chunk:chunk + POOL_HALO, :]


def _mixer(proj, dmask, qd, kd, cd, pool_w, pool_scale, *, batch, lp, chunk):
    nc = lp // chunk
    m = proj.shape[0]
    kern = functools.partial(_mixer_kernel, chunk=chunk)
    const3 = lambda b, c: (0, 0, 0)
    col_block = lambda n: pl.BlockSpec((chunk, RET_W), lambda b, c: (b * nc + c, n))
    return pl.pallas_call(
        kern,
        grid=(batch, nc),
        in_specs=[
            col_block(0), col_block(1), col_block(2), col_block(3), col_block(4),
            pl.BlockSpec((RET_HEADS, chunk, chunk), const3),
            pl.BlockSpec((RET_HEADS, chunk, 1), const3),
            pl.BlockSpec((RET_HEADS, chunk, 1), const3),
            pl.BlockSpec((RET_HEADS, 1, 1), const3),
            pl.BlockSpec((len(POOL_WINDOWS), POOL_GROUP, POOL_GROUP), const3),
            pl.BlockSpec((1, POOL_W), lambda b, c: (0, 0)),
        ],
        out_specs=pl.BlockSpec((chunk, RET_W + POOL_W), lambda b, c: (b * nc + c, 0)),
        out_shape=jax.ShapeDtypeStruct((m, RET_W + POOL_W), BF16),
        scratch_shapes=[
            pltpu.VMEM((RET_HEADS, HEAD_DIM, HEAD_DIM), F32),
            pltpu.VMEM((POOL_HALO + chunk, POOL_W), F32),
        ],
        compiler_params=_params(("arbitrary", "arbitrary")),
        name="mixer",
    )(proj, proj, proj, proj, proj, dmask, qd, kd, cd, pool_w, pool_scale)


def _proj_res_kernel(lhs_ref, w_ref, res_ref, out_ref, *, parts):
    w = w_ref[...]
    if w.dtype != BF16:
        w = w.astype(BF16)
    tp = lhs_ref.shape[0] // parts
    for p in range(parts):
        rows = slice(p * tp, (p + 1) * tp)
        out_ref[rows, :] = res_ref[rows, :] + jnp.dot(lhs_ref[rows, :], w, preferred_element_type=F32)


def _proj_res(lhs, w, res, *, layer, tm, tn, parts, name):
    m, k = lhs.shape
    n = w.shape[2]
    assert tm % (16 * parts) == 0
    return pl.pallas_call(
        functools.partial(_proj_res_kernel, parts=parts),
        grid=(m // tm, n // tn),
        in_specs=[
            pl.BlockSpec((tm, k), lambda i, j: (i, 0), pipeline_mode=pl.Buffered(1)),
            pl.BlockSpec((None, k, tn), lambda i, j: (layer, 0, j)),
            pl.BlockSpec((tm, tn), lambda i, j: (i, j)),
        ],
        out_specs=pl.BlockSpec((tm, tn), lambda i, j: (i, j)),
        out_shape=jax.ShapeDtypeStruct((m, n), F32),
        compiler_params=_params(("arbitrary", "arbitrary")),
        name=name,
    )(lhs, w, res)


def _ffn_up_kernel(h_ref, g_ref, wa_ref, wb_ref, cw_ref, cb_ref, act_ref, hn_ref, aext_ref, carry_ref, *, rc, tm):
    i = pl.program_id(0)
    jj = pl.program_id(1)
    _norm_phase(jj, h_ref, g_ref, hn_ref, rc)

    @pl.when(jj >= NORM_CHUNKS)
    def _():
        j = jj - NORM_CHUNKS
        @pl.when(i == 0)
        def _():
            carry_ref[j] = jnp.zeros(carry_ref.shape[1:], F32)

        wa = wa_ref[...].astype(BF16)
        wb = wb_ref[...].astype(BF16)
        aext_ref[0:CONV_HALO, :] = carry_ref[j]
        tp = tm // ROW_PARTS
        for p in range(ROW_PARTS):
            rows = slice(p * tp, (p + 1) * tp)
            hn = hn_ref[rows, :]
            a = jnp.dot(hn, wa, preferred_element_type=F32)
            b = jnp.dot(hn, wb, preferred_element_type=F32)
            r0 = CONV_HALO + p * tp
            aext_ref[r0:r0 + tp, :] = a
            ac = cb_ref[...] + a * cw_ref[CONV_W - 1:CONV_W, :]
            for tap in range(CONV_W - 1):
                back = CONV_W - 1 - tap
                ac = ac + aext_ref[r0 - back:r0 - back + tp, :] * cw_ref[tap:tap + 1, :]
            act_ref[rows, :] = (ac * jax.nn.sigmoid(ac) * b).astype(BF16)
        carry_ref[j] = aext_ref[tm:tm + CONV_HALO, :]


def _ffn_up(h, g, w_up, cw, cb, *, layer, tm, tn):
    m = h.shape[0]
    n_j = D_FF // tn
    rc = tm // NORM_CHUNKS
    kern = functools.partial(_ffn_up_kernel, rc=rc, tm=tm)
    h_spec, g_spec = _norm_specs(tm, rc)
    return pl.pallas_call(
        kern,
        grid=(m // tm, NORM_CHUNKS + n_j),
        in_specs=[
            h_spec,
            g_spec,
            pl.BlockSpec((None, D_MODEL, tn), lambda i, jj: (layer, 0, _col(jj))),
            pl.BlockSpec((None, D_MODEL, tn), lambda i, jj: (layer, 0, n_j + _col(jj))),
            pl.BlockSpec((CONV_W, tn), lambda i, jj: (0, _col(jj))),
            pl.BlockSpec((1, tn), lambda i, jj: (0, _col(jj))),
        ],
        out_specs=pl.BlockSpec((tm, tn), lambda i, jj: (i, _col(jj))),
        out_shape=jax.ShapeDtypeStruct((m, D_FF), BF16),
        scratch_shapes=[
            pltpu.VMEM((tm, D_MODEL), BF16),
            pltpu.VMEM((CONV_HALO + tm, tn), F32),
            pltpu.VMEM((n_j, CONV_HALO, tn), F32),
        ],
        compiler_params=_params(("arbitrary", "arbitrary")),
        name="ffn_up",
    )(h, g, w_up, w_up, cw, cb)


def _final_norm_kernel(h_ref, g_ref, out_ref):
    out_ref[0] = _rms_rows(h_ref[...], g_ref[...])


def _final_norm(h, g, *, batch, seq, lp, tr):
    skip = (META_PAD + N_META) // tr
    per_b = lp // tr
    return pl.pallas_call(
        _final_norm_kernel,
        grid=(batch, seq // tr),
        in_specs=[
            pl.BlockSpec((tr, D_MODEL), lambda b, c: (b * per_b + skip + c, 0)),
            pl.BlockSpec((1, D_MODEL), lambda b, c: (0, 0)),
        ],
        out_specs=pl.BlockSpec((1, tr, D_MODEL), lambda b, c: (b, c, 0)),
        out_shape=jax.ShapeDtypeStruct((batch, seq, D_MODEL), F32),
        compiler_params=_params(("arbitrary", "arbitrary")),
        name="final_norm",
    )(h, g)


def _decay_tables(chunk):
    log_gamma = jnp.log1p(-jnp.exp2(-5.0 - jnp.arange(RET_HEADS, dtype=F32)))
    idx = jnp.arange(chunk, dtype=F32)
    rel = idx[:, None] - idx[None, :]
    dmask = jnp.where(rel[None] >= 0, jnp.exp(log_gamma[:, None, None] * jnp.maximum(rel, 0.0)[None]), 0.0)
    qd = jnp.exp(log_gamma[:, None] * (idx + 1.0)[None, :])[:, :, None]
    kd = jnp.exp(log_gamma[:, None] * (chunk - 1.0 - idx)[None, :])[:, :, None]
    cd = jnp.exp(log_gamma * chunk)[:, None, None]
    return dmask, qd, kd, cd


def kernel(x, meta_tokens, norm1_g, w_in, pool_w, pool_scale, w_out, norm2_g, w_up, conv_w, conv_b, w_down, final_g):
    batch, seq, d = x.shape
    depth = w_in.shape[0]
    assert d == D_MODEL and meta_tokens.shape == (N_META, D_MODEL)
    assert w_up.shape[2] == 2 * D_FF and w_down.shape[1] == D_FF
    lp = META_PAD + N_META + seq
    m = batch * lp
    tm_norm = m // 6
    tm_out = m // 3
    tm_down = m // 8
    chunk = 384
    assert lp % chunk == 0
    assert tm_norm % (16 * NORM_CHUNKS) == 0 and tm_norm % (16 * ROW_PARTS) == 0
    for t in (tm_norm, tm_out, tm_down):
        assert m % t == 0 and t % 16 == 0

    meta = jnp.broadcast_to(meta_tokens[None].astype(x.dtype), (batch, N_META, d))
    h = jnp.concatenate([jnp.zeros((batch, META_PAD, d), x.dtype), meta, x], axis=1).reshape(m, d)

    pos = (jnp.arange(lp) - META_PAD).astype(F32)
    inv = jnp.power(ROPE_BASE, -jnp.arange(HALF, dtype=F32) / HALF)
    ang = pos[:, None] * inv[None, :]
    cos = jnp.tile(jnp.cos(ang), (batch, 1))
    sin = jnp.tile(jnp.sin(ang), (batch, 1))
    dmask, qd, kd, cd = _decay_tables(chunk)

    for l in range(depth):
        proj = _in_proj(h, norm1_g[l][None], w_in, cos, sin, layer=l, tm=tm_norm, tn=512)
        mixed = _mixer(proj, dmask, qd, kd, cd, pool_w[l].astype(BF16), pool_scale[l][None],
                       batch=batch, lp=lp, chunk=chunk)
        h = _proj_res(mixed, w_out, h, layer=l, tm=tm_out, tn=256, parts=4, name="out_proj")
        act = _ffn_up(h, norm2_g[l][None], w_up, conv_w[l], conv_b[l][None], layer=l, tm=tm_norm, tn=256)
        h = _proj_res(act, w_down[l].astype(BF16)[None], h, layer=0, tm=tm_down, tn=256, parts=2, name="ffn_down")

    return _final_norm(h, final_g[None], batch=batch, seq=seq, lp=lp, tr=ROW_ALIGN)
```

```python
import functools

import jax
import jax.numpy as jnp
from jax import lax
from jax.experimental import pallas as pl
from jax.experimental.pallas import tpu as pltpu

D_MODEL = 4096
N_META = 16
RET_W = 2048
POOL_W = 2048
RET_HEADS = 8
HEAD_DIM = 256
HALF = HEAD_DIM // 2
POOL_WINDOWS = (2, 4, 8, 16)
POOL_GROUP = 512
D_FF = 11008
CONV_W = 3
ROPE_BASE = 10000.0
EPS = 1e-6

LANE = 128
BF16_ROWS = 16
ROW_ALIGN = 128
META_PAD = (-N_META) % ROW_ALIGN
POOL_HALO = 16
CONV_HALO = 8
NORM_CHUNKS = 12
ROW_PARTS = 4
FFN_ROW_PARTS = 6
VMEM_LIMIT = 58 * 1024 * 1024

BF16 = jnp.bfloat16
F32 = jnp.float32


def _params(sem):
    return pltpu.CompilerParams(dimension_semantics=sem, vmem_limit_bytes=VMEM_LIMIT)


def _rms_rows(x, g):
    return x * lax.rsqrt(jnp.mean(x * x, axis=-1, keepdims=True) + EPS) * g


def _norm_phase(jj, h_ref, g_ref, hn_ref, rc):
    @pl.when(jj < NORM_CHUNKS)
    def _():
        r0 = pl.multiple_of(jj * rc, 16)
        hn_ref[pl.ds(r0, rc), :] = _rms_rows(h_ref[...], g_ref[...]).astype(BF16)


def _norm_specs(tm, rc):
    h_spec = pl.BlockSpec((rc, D_MODEL), lambda i, jj: (i * NORM_CHUNKS + jnp.minimum(jj, NORM_CHUNKS - 1), 0))
    g_spec = pl.BlockSpec((1, D_MODEL), lambda i, jj: (0, 0))
    return h_spec, g_spec


def _col(jj):
    return jnp.maximum(jj - NORM_CHUNKS, 0)


def _in_proj_kernel(h_ref, g_ref, w_ref, cos_ref, sin_ref, out_ref, hn_ref, *, rc, tn, n_q, n_rot):
    jj = pl.program_id(1)
    _norm_phase(jj, h_ref, g_ref, hn_ref, rc)

    @pl.when(jj >= NORM_CHUNKS)
    def _():
        j = jj - NORM_CHUNKS
        w = w_ref[...].astype(BF16)
        is_rot = j < n_rot
        scale = jnp.where(jnp.logical_and(j >= n_q, is_rot), HEAD_DIM ** -0.5, 1.0).astype(F32)
        tp = hn_ref.shape[0] // ROW_PARTS
        for p in range(ROW_PARTS):
            rows = slice(p * tp, (p + 1) * tp)
            acc = jnp.dot(hn_ref[rows, :], w, preferred_element_type=F32)
            cos = jnp.where(is_rot, cos_ref[rows, :], 1.0) * scale
            sin = jnp.where(is_rot, sin_ref[rows, :], 0.0) * scale
            for hh in range(tn // HEAD_DIM):
                x1 = acc[:, hh * HEAD_DIM: hh * HEAD_DIM + HALF]
                x2 = acc[:, hh * HEAD_DIM + HALF: (hh + 1) * HEAD_DIM]
                out_ref[rows, hh * HEAD_DIM: hh * HEAD_DIM + HALF] = (x1 * cos - x2 * sin).astype(BF16)
                out_ref[rows, hh * HEAD_DIM + HALF: (hh + 1) * HEAD_DIM] = (x1 * sin + x2 * cos).astype(BF16)


def _in_proj(h, g, w, cos, sin, *, layer, tm, tn):
    m = h.shape[0]
    n = w.shape[2]
    rc = tm // NORM_CHUNKS
    kern = functools.partial(_in_proj_kernel, rc=rc, tn=tn, n_q=RET_W // tn, n_rot=2 * RET_W // tn)
    h_spec, g_spec = _norm_specs(tm, rc)
    return pl.pallas_call(
        kern,
        grid=(m // tm, NORM_CHUNKS + n // tn),
        in_specs=[
            h_spec,
            g_spec,
            pl.BlockSpec((None, D_MODEL, tn), lambda i, jj: (layer, 0, _col(jj))),
            pl.BlockSpec((tm, HALF), lambda i, jj: (i, 0)),
            pl.BlockSpec((tm, HALF), lambda i, jj: (i, 0)),
        ],
        out_specs=pl.BlockSpec((tm, tn), lambda i, jj: (i, _col(jj))),
        out_shape=jax.ShapeDtypeStruct((m, n), BF16),
        scratch_shapes=[pltpu.VMEM((tm, D_MODEL), BF16)],
        compiler_params=_params(("arbitrary", "arbitrary")),
        name="in_proj",
    )(h, g, w, cos, sin)


def _mixer_kernel(q_ref, k_ref, v_ref, g_ref, p_ref, dmask_ref, qd_ref, kd_ref, cd_ref, pw_ref, ps_ref,
                  out_ref, state_ref, pext_ref, *, chunk):
    c = pl.program_id(1)

    @pl.when(c == 0)
    def _():
        state_ref[...] = jnp.zeros_like(state_ref)
        pext_ref[0:POOL_HALO, :] = jnp.zeros((POOL_HALO, POOL_W), F32)

    for hh in range(RET_HEADS):
        sl = slice(hh * HEAD_DIM, (hh + 1) * HEAD_DIM)
        q = q_ref[:, sl]
        k = k_ref[:, sl]
        v = v_ref[:, sl]
        scores = lax.dot_general(q, k, (((1,), (1,)), ((), ())), preferred_element_type=F32)
        scores = (scores * dmask_ref[hh]).astype(BF16)
        intra = jnp.dot(scores, v, preferred_element_type=F32)
        state = state_ref[hh]
        cross = jnp.dot(q, state.astype(BF16), preferred_element_type=F32) * qd_ref[hh]
        kdec = (k.astype(F32) * kd_ref[hh]).astype(BF16)
        upd = lax.dot_general(kdec, v, (((0,), (0,)), ((), ())), preferred_element_type=F32)
        state_ref[hh] = state * cd_ref[hh] + upd
        r = intra + cross
        r = r * lax.rsqrt(jnp.mean(r * r, axis=-1, keepdims=True) + EPS)
        gate = g_ref[:, sl].astype(F32)
        out_ref[:, sl] = (r * (gate * jax.nn.sigmoid(gate))).astype(BF16)

    pext_ref[POOL_HALO:POOL_HALO + chunk, :] = p_ref[...].astype(F32)
    t = c * chunk - META_PAD + lax.broadcasted_iota(jnp.int32, (chunk, 1), 0)
    valid = t >= 0
    for gi, w in enumerate(POOL_WINDOWS):
        cs = slice(gi * POOL_GROUP, (gi + 1) * POOL_GROUP)
        xg = pext_ref[POOL_HALO:POOL_HALO + chunk, cs]
        win = xg
        for d in range(1, w):
            win = win + pext_ref[POOL_HALO - d:POOL_HALO - d + chunk, cs]
        cnt = jnp.clip(t + 1, 1, w).astype(F32)
        mix = jnp.where(valid, win / cnt - xg, 0.0).astype(BF16)
        mg = jnp.dot(mix, pw_ref[gi], preferred_element_type=F32) * ps_ref[:, cs]
        out_ref[:, RET_W + gi * POOL_GROUP: RET_W + (gi + 1) * POOL_GROUP] = mg.astype(BF16)
    pext_ref[0:POOL_HALO, :] = pext_ref[chunk:chunk + POOL_HALO, :]


def _mixer(proj, dmask, qd, kd, cd, pool_w, pool_scale, *, batch, lp, chunk):
    nc = lp // chunk
    m = proj.shape[0]
    kern = functools.partial(_mixer_kernel, chunk=chunk)
    const3 = lambda b, c: (0, 0, 0)
    col_block = lambda n: pl.BlockSpec((chunk, RET_W), lambda b, c: (b * nc + c, n))
    return pl.pallas_call(
        kern,
        grid=(batch, nc),
        in_specs=[
            col_block(0), col_block(1), col_block(2), col_block(3), col_block(4),
            pl.BlockSpec((RET_HEADS, chunk, chunk), const3),
            pl.BlockSpec((RET_HEADS, chunk, 1), const3),
            pl.BlockSpec((RET_HEADS, chunk, 1), const3),
            pl.BlockSpec((RET_HEADS, 1, 1), const3),
            pl.BlockSpec((len(POOL_WINDOWS), POOL_GROUP, POOL_GROUP), const3),
            pl.BlockSpec((1, POOL_W), lambda b, c: (0, 0)),
        ],
        out_specs=pl.BlockSpec((chunk, RET_W + POOL_W), lambda b, c: (b * nc + c, 0)),
        out_shape=jax.ShapeDtypeStruct((m, RET_W + POOL_W), BF16),
        scratch_shapes=[
            pltpu.VMEM((RET_HEADS, HEAD_DIM, HEAD_DIM), F32),
            pltpu.VMEM((POOL_HALO + chunk, POOL_W), F32),
        ],
        compiler_params=_params(("arbitrary", "arbitrary")),
        name="mixer",
    )(proj, proj, proj, proj, proj, dmask, qd, kd, cd, pool_w, pool_scale)


def _proj_res_kernel(lhs_ref, w_ref, res_ref, out_ref, *, parts):
    w = w_ref[...].astype(BF16)
    tp = lhs_ref.shape[0] // parts
    for p in range(parts):
        rows = slice(p * tp, (p + 1) * tp)
        out_ref[rows, :] = res_ref[rows, :] + jnp.dot(lhs_ref[rows, :], w, preferred_element_type=F32)


def _proj_res(lhs, w, res, *, layer, tm, tn, parts, name, k_blocks=1, k_block=0):
    m, k = lhs.shape
    n = w.shape[2]
    kb = k // k_blocks
    assert tm % (BF16_ROWS * parts) == 0 and k % k_blocks == 0 and kb % LANE == 0
    return pl.pallas_call(
        functools.partial(_proj_res_kernel, parts=parts),
        grid=(m // tm, n // tn),
        in_specs=[
            pl.BlockSpec((tm, kb), lambda i, j: (i, k_block), pipeline_mode=pl.Buffered(1)),
            pl.BlockSpec((None, kb, tn), lambda i, j: (layer, k_block, j)),
            pl.BlockSpec((tm, tn), lambda i, j: (i, j)),
        ],
        out_specs=pl.BlockSpec((tm, tn), lambda i, j: (i, j)),
        out_shape=jax.ShapeDtypeStruct((m, n), F32),
        compiler_params=_params(("arbitrary", "arbitrary")),
        name=name,
    )(lhs, w, res)


def _ffn_up_kernel(h_ref, g_ref, wa_ref, wb_ref, cw_ref, cb_ref, act_ref, hn_ref, aext_ref, carry_ref, *, rc, tm):
    i = pl.program_id(0)
    jj = pl.program_id(1)
    _norm_phase(jj, h_ref, g_ref, hn_ref, rc)

    @pl.when(jj >= NORM_CHUNKS)
    def _():
        j = jj - NORM_CHUNKS
        @pl.when(i == 0)
        def _():
            carry_ref[j] = jnp.zeros(carry_ref.shape[1:], F32)

        wa = wa_ref[...].astype(BF16)
        wb = wb_ref[...].astype(BF16)
        aext_ref[0:CONV_HALO, :] = carry_ref[j]
        tp = tm // FFN_ROW_PARTS
        for p in range(FFN_ROW_PARTS):
            rows = slice(p * tp, (p + 1) * tp)
            hn = hn_ref[rows, :]
            a = jnp.dot(hn, wa, preferred_element_type=F32)
            b = jnp.dot(hn, wb, preferred_element_type=F32)
            r0 = CONV_HALO + p * tp
            aext_ref[r0:r0 + tp, :] = a
            ac = cb_ref[...] + a * cw_ref[CONV_W - 1:CONV_W, :]
            for tap in range(CONV_W - 1):
                back = CONV_W - 1 - tap
                ac = ac + aext_ref[r0 - back:r0 - back + tp, :] * cw_ref[tap:tap + 1, :]
            act_ref[rows, :] = (ac * jax.nn.sigmoid(ac) * b).astype(BF16)
        carry_ref[j] = aext_ref[tm:tm + CONV_HALO, :]


def _ffn_up(h, g, w_up, cw, cb, *, layer, tm, tn):
    m = h.shape[0]
    n_j = D_FF // tn
    rc = tm // NORM_CHUNKS
    kern = functools.partial(_ffn_up_kernel, rc=rc, tm=tm)
    h_spec, g_spec = _norm_specs(tm, rc)
    return pl.pallas_call(
        kern,
        grid=(m // tm, NORM_CHUNKS + n_j),
        in_specs=[
            h_spec,
            g_spec,
            pl.BlockSpec((None, D_MODEL, tn), lambda i, jj: (layer, 0, _col(jj))),
            pl.BlockSpec((None, D_MODEL, tn), lambda i, jj: (layer, 0, n_j + _col(jj))),
            pl.BlockSpec((CONV_W, tn), lambda i, jj: (0, _col(jj))),
            pl.BlockSpec((1, tn), lambda i, jj: (0, _col(jj))),
        ],
        out_specs=pl.BlockSpec((tm, tn), lambda i, jj: (i, _col(jj))),
        out_shape=jax.ShapeDtypeStruct((m, D_FF), BF16),
        scratch_shapes=[
            pltpu.VMEM((tm, D_MODEL), BF16),
            pltpu.VMEM((CONV_HALO + tm, tn), F32),
            pltpu.VMEM((n_j, CONV_HALO, tn), F32),
        ],
        compiler_params=_params(("arbitrary", "arbitrary")),
        name="ffn_up",
    )(h, g, w_up, w_up, cw, cb)


def _final_norm_kernel(h_ref, g_ref, out_ref):
    out_ref[0] = _rms_rows(h_ref[...], g_ref[...])


def _final_norm(h, g, *, batch, seq, lp, tr):
    skip = (META_PAD + N_META) // tr
    per_b = lp // tr
    return pl.pallas_call(
        _final_norm_kernel,
        grid=(batch, seq // tr),
        in_specs=[
            pl.BlockSpec((tr, D_MODEL), lambda b, c: (b * per_b + skip + c, 0)),
            pl.BlockSpec((1, D_MODEL), lambda b, c: (0, 0)),
        ],
        out_specs=pl.BlockSpec((1, tr, D_MODEL), lambda b, c: (b, c, 0)),
        out_shape=jax.ShapeDtypeStruct((batch, seq, D_MODEL), F32),
        compiler_params=_params(("arbitrary", "arbitrary")),
        name="final_norm",
    )(h, g)


def _decay_tables(chunk):
    log_gamma = jnp.log1p(-jnp.exp2(-5.0 - jnp.arange(RET_HEADS, dtype=F32)))
    idx = jnp.arange(chunk, dtype=F32)
    rel = idx[:, None] - idx[None, :]
    dmask = jnp.where(rel[None] >= 0, jnp.exp(log_gamma[:, None, None] * jnp.maximum(rel, 0.0)[None]), 0.0)
    qd = jnp.exp(log_gamma[:, None] * (idx + 1.0)[None, :])[:, :, None]
    kd = jnp.exp(log_gamma[:, None] * (chunk - 1.0 - idx)[None, :])[:, :, None]
    cd = jnp.exp(log_gamma * chunk)[:, None, None]
    return dmask, qd, kd, cd


def kernel(x, meta_tokens, norm1_g, w_in, pool_w, pool_scale, w_out, norm2_g, w_up, conv_w, conv_b, w_down, final_g):
    batch, seq, d = x.shape
    depth = w_in.shape[0]
    assert d == D_MODEL and meta_tokens.shape == (N_META, D_MODEL)
    assert w_up.shape[2] == 2 * D_FF and w_down.shape[1] == D_FF
    lp = META_PAD + N_META + seq
    m = batch * lp
    tm_norm = m // 4
    tm_out = m // 3
    tm_down = m // 4
    chunk = 384
    assert lp % chunk == 0
    for parts in (NORM_CHUNKS, ROW_PARTS, FFN_ROW_PARTS):
        assert tm_norm % (BF16_ROWS * parts) == 0
    for t in (tm_norm, tm_out, tm_down):
        assert m % t == 0

    meta = jnp.broadcast_to(meta_tokens[None].astype(x.dtype), (batch, N_META, d))
    h = jnp.concatenate([jnp.zeros((batch, META_PAD, d), x.dtype), meta, x], axis=1).reshape(m, d)

    pos = (jnp.arange(lp) - META_PAD).astype(F32)
    inv = jnp.power(ROPE_BASE, -jnp.arange(HALF, dtype=F32) / HALF)
    ang = pos[:, None] * inv[None, :]
    cos = jnp.tile(jnp.cos(ang), (batch, 1))
    sin = jnp.tile(jnp.sin(ang), (batch, 1))
    dmask, qd, kd, cd = _decay_tables(chunk)

    for l in range(depth):
        proj = _in_proj(h, norm1_g[l][None], w_in, cos, sin, layer=l, tm=tm_norm, tn=512)
        mixed = _mixer(proj, dmask, qd, kd, cd, pool_w[l].astype(BF16), pool_scale[l][None],
                       batch=batch, lp=lp, chunk=chunk)
        h = _proj_res(mixed, w_out, h, layer=l, tm=tm_out, tn=256, parts=4, name="out_proj")
        act = _ffn_up(h, norm2_g[l][None], w_up, conv_w[l], conv_b[l][None], layer=l, tm=tm_norm, tn=256)
        for kb in range(2):
            h = _proj_res(act, w_down, h, layer=l, tm=tm_down, tn=256, parts=4, name="ffn_down",
                          k_blocks=2, k_block=kb)

    return _final_norm(h, final_g[None], batch=batch, seq=seq, lp=lp, tr=ROW_ALIGN)
```

```python
import functools

import jax
import jax.numpy as jnp
from jax import lax
from jax.experimental import pallas as pl
from jax.experimental.pallas import tpu as pltpu

D_MODEL = 4096
N_META = 16
RET_W = 2048
POOL_W = 2048
RET_HEADS = 8
HEAD_DIM = 256
HALF = HEAD_DIM // 2
POOL_WINDOWS = (2, 4, 8, 16)
POOL_GROUP = 512
D_FF = 11008
CONV_W = 3
ROPE_BASE = 10000.0
EPS = 1e-6

LANE = 128
BF16_ROWS = 16
ROW_ALIGN = 128
META_PAD = (-N_META) % ROW_ALIGN
POOL_HALO = 128
CONV_HALO = 8
NORM_CHUNKS = 12
VMEM_LIMIT = 58 * 1024 * 1024

BF16 = jnp.bfloat16
F32 = jnp.float32


def _params(sem):
    return pltpu.CompilerParams(dimension_semantics=sem, vmem_limit_bytes=VMEM_LIMIT)


def _rms_rows(x, g):
    return x * lax.rsqrt(jnp.mean(x * x, axis=-1, keepdims=True) + EPS) * g


def _norm_phase(jj, h_ref, g_ref, hn_ref, rc):
    @pl.when(jj < NORM_CHUNKS)
    def _():
        r0 = pl.multiple_of(jj * rc, 16)
        hn_ref[pl.ds(r0, rc), :] = _rms_rows(h_ref[...], g_ref[...]).astype(BF16)


def _norm_specs(tm, rc):
    h_spec = pl.BlockSpec((rc, D_MODEL), lambda i, jj: (i * NORM_CHUNKS + jnp.minimum(jj, NORM_CHUNKS - 1), 0))
    g_spec = pl.BlockSpec((1, D_MODEL), lambda i, jj: (0, 0))
    return h_spec, g_spec


def _col(jj):
    return jnp.maximum(jj - NORM_CHUNKS, 0)


def _in_proj_kernel(h_ref, g_ref, w_ref, cos_ref, sin_ref, out_ref, hn_ref, *, rc, tn, n_q, n_rot, parts):
    jj = pl.program_id(1)
    _norm_phase(jj, h_ref, g_ref, hn_ref, rc)

    @pl.when(jj >= NORM_CHUNKS)
    def _():
        j = jj - NORM_CHUNKS
        w = w_ref[...].astype(BF16)
        is_rot = j < n_rot
        scale = jnp.where(jnp.logical_and(j >= n_q, is_rot), HEAD_DIM ** -0.5, 1.0).astype(F32)
        tp = hn_ref.shape[0] // parts
        for p in range(parts):
            rows = slice(p * tp, (p + 1) * tp)
            acc = jnp.dot(hn_ref[rows, :], w, preferred_element_type=F32)
            cos = jnp.where(is_rot, cos_ref[rows, :], 1.0) * scale
            sin = jnp.where(is_rot, sin_ref[rows, :], 0.0) * scale
            for hh in range(tn // HEAD_DIM):
                x1 = acc[:, hh * HEAD_DIM: hh * HEAD_DIM + HALF]
                x2 = acc[:, hh * HEAD_DIM + HALF: (hh + 1) * HEAD_DIM]
                out_ref[rows, hh * HEAD_DIM: hh * HEAD_DIM + HALF] = (x1 * cos - x2 * sin).astype(BF16)
                out_ref[rows, hh * HEAD_DIM + HALF: (hh + 1) * HEAD_DIM] = (x1 * sin + x2 * cos).astype(BF16)


def _in_proj(h, g, w, cos, sin, *, layer, tm, tn, parts):
    m = h.shape[0]
    n = w.shape[2]
    rc = tm // NORM_CHUNKS
    assert tm % (BF16_ROWS * parts) == 0
    kern = functools.partial(_in_proj_kernel, rc=rc, tn=tn, n_q=RET_W // tn, n_rot=2 * RET_W // tn, parts=parts)
    h_spec, g_spec = _norm_specs(tm, rc)
    return pl.pallas_call(
        kern,
        grid=(m // tm, NORM_CHUNKS + n // tn),
        in_specs=[
            h_spec,
            g_spec,
            pl.BlockSpec((None, D_MODEL, tn), lambda i, jj: (layer, 0, _col(jj))),
            pl.BlockSpec((tm, HALF), lambda i, jj: (i, 0)),
            pl.BlockSpec((tm, HALF), lambda i, jj: (i, 0)),
        ],
        out_specs=pl.BlockSpec((tm, tn), lambda i, jj: (i, _col(jj))),
        out_shape=jax.ShapeDtypeStruct((m, n), BF16),
        scratch_shapes=[pltpu.VMEM((tm, D_MODEL), BF16)],
        compiler_params=_params(("arbitrary", "arbitrary")),
        name="in_proj",
    )(h, g, w, cos, sin)


def _mixer_kernel(q_ref, k_ref, v_ref, g_ref, p_ref, dmask_ref, qd_ref, kd_ref, cd_ref, band_ref, pw_ref, ps_ref,
                  out_ref, state_ref, pext_ref, *, chunk):
    c = pl.program_id(1)

    @pl.when(c == 0)
    def _():
        state_ref[...] = jnp.zeros_like(state_ref)
        pext_ref[0:POOL_HALO, :] = jnp.zeros((POOL_HALO, POOL_W), BF16)

    for hh in range(RET_HEADS):
        sl = slice(hh * HEAD_DIM, (hh + 1) * HEAD_DIM)
        q = q_ref[:, sl]
        k = k_ref[:, sl]
        v = v_ref[:, sl]
        scores = lax.dot_general(q, k, (((1,), (1,)), ((), ())), preferred_element_type=F32)
        scores = (scores * dmask_ref[hh]).astype(BF16)
        intra = jnp.dot(scores, v, preferred_element_type=F32)
        state = state_ref[hh]
        cross = jnp.dot(q, state.astype(BF16), preferred_element_type=F32) * qd_ref[hh]
        kdec = (k.astype(F32) * kd_ref[hh]).astype(BF16)
        upd = lax.dot_general(kdec, v, (((0,), (0,)), ((), ())), preferred_element_type=F32)
        state_ref[hh] = state * cd_ref[hh] + upd
        r = intra + cross
        r = r * lax.rsqrt(jnp.mean(r * r, axis=-1, keepdims=True) + EPS)
        gate = g_ref[:, sl].astype(F32)
        out_ref[:, sl] = (r * (gate * jax.nn.sigmoid(gate))).astype(BF16)

    pext_ref[POOL_HALO:POOL_HALO + chunk, :] = p_ref[...]
    t = c * chunk - META_PAD + lax.broadcasted_iota(jnp.int32, (chunk, 1), 0)
    valid = t >= 0
    for gi, w in enumerate(POOL_WINDOWS):
        cs = slice(gi * POOL_GROUP, (gi + 1) * POOL_GROUP)
        xg = p_ref[:, cs].astype(F32)
        win = jnp.dot(band_ref[gi], pext_ref[:, cs], preferred_element_type=F32)
        cnt = jnp.clip(t + 1, 1, w).astype(F32)
        mix = jnp.where(valid, win / cnt - xg, 0.0).astype(BF16)
        mg = jnp.dot(mix, pw_ref[gi], preferred_element_type=F32) * ps_ref[:, cs]
        out_ref[:, RET_W + gi * POOL_GROUP: RET_W + (gi + 1) * POOL_GROUP] = mg.astype(BF16)
    pext_ref[0:POOL_HALO, :] = pext_ref[chunk:chunk + POOL_HALO, :]


def _mixer(proj, dmask, qd, kd, cd, band, pool_w, pool_scale, *, batch, lp, chunk):
    nc = lp // chunk
    m = proj.shape[0]
    assert chunk >= POOL_HALO
    kern = functools.partial(_mixer_kernel, chunk=chunk)
    const3 = lambda b, c: (0, 0, 0)
    col_block = lambda n: pl.BlockSpec((chunk, RET_W), lambda b, c: (b * nc + c, n))
    return pl.pallas_call(
        kern,
        grid=(batch, nc),
        in_specs=[
            col_block(0), col_block(1), col_block(2), col_block(3), col_block(4),
            pl.BlockSpec((RET_HEADS, chunk, chunk), const3),
            pl.BlockSpec((RET_HEADS, chunk, 1), const3),
            pl.BlockSpec((RET_HEADS, chunk, 1), const3),
            pl.BlockSpec((RET_HEADS, 1, 1), const3),
            pl.BlockSpec((len(POOL_WINDOWS), chunk, POOL_HALO + chunk), const3),
            pl.BlockSpec((len(POOL_WINDOWS), POOL_GROUP, POOL_GROUP), const3),
            pl.BlockSpec((1, POOL_W), lambda b, c: (0, 0)),
        ],
        out_specs=pl.BlockSpec((chunk, RET_W + POOL_W), lambda b, c: (b * nc + c, 0)),
        out_shape=jax.ShapeDtypeStruct((m, RET_W + POOL_W), BF16),
        scratch_shapes=[
            pltpu.VMEM((RET_HEADS, HEAD_DIM, HEAD_DIM), F32),
            pltpu.VMEM((POOL_HALO + chunk, POOL_W), BF16),
        ],
        compiler_params=_params(("arbitrary", "arbitrary")),
        name="mixer",
    )(proj, proj, proj, proj, proj, dmask, qd, kd, cd, band, pool_w, pool_scale)


def _proj_res_kernel(lhs_ref, w_ref, res_ref, out_ref, *, parts):
    w = w_ref[...].astype(BF16)
    tp = lhs_ref.shape[0] // parts
    for p in range(parts):
        rows = slice(p * tp, (p + 1) * tp)
        out_ref[rows, :] = res_ref[rows, :] + jnp.dot(lhs_ref[rows, :], w, preferred_element_type=F32)


def _proj_res(lhs, w, res, *, layer, tm, tn, parts, name, k_blocks=1, k_block=0):
    m, k = lhs.shape
    n = w.shape[2]
    kb = k // k_blocks
    assert tm % (BF16_ROWS * parts) == 0 and k % k_blocks == 0 and kb % LANE == 0
    return pl.pallas_call(
        functools.partial(_proj_res_kernel, parts=parts),
        grid=(m // tm, n // tn),
        in_specs=[
            pl.BlockSpec((tm, kb), lambda i, j: (i, k_block), pipeline_mode=pl.Buffered(1)),
            pl.BlockSpec((None, kb, tn), lambda i, j: (layer, k_block, j)),
            pl.BlockSpec((tm, tn), lambda i, j: (i, j)),
        ],
        out_specs=pl.BlockSpec((tm, tn), lambda i, j: (i, j)),
        out_shape=jax.ShapeDtypeStruct((m, n), F32),
        compiler_params=_params(("arbitrary", "arbitrary")),
        name=name,
    )(lhs, w, res)


def _ffn_up_kernel(h_ref, g_ref, wa_ref, wb_ref, cw_ref, cb_ref, act_ref, hn_ref, aext_ref, carry_ref,
                   *, rc, tm, parts):
    i = pl.program_id(0)
    jj = pl.program_id(1)
    _norm_phase(jj, h_ref, g_ref, hn_ref, rc)

    @pl.when(jj >= NORM_CHUNKS)
    def _():
        j = jj - NORM_CHUNKS
        @pl.when(i == 0)
        def _():
            carry_ref[j] = jnp.zeros(carry_ref.shape[1:], F32)

        wa = wa_ref[...].astype(BF16)
        wb = wb_ref[...].astype(BF16)
        aext_ref[0:CONV_HALO, :] = carry_ref[j]
        tp = tm // parts
        for p in range(parts):
            rows = slice(p * tp, (p + 1) * tp)
            hn = hn_ref[rows, :]
            a = jnp.dot(hn, wa, preferred_element_type=F32)
            b = jnp.dot(hn, wb, preferred_element_type=F32)
            r0 = CONV_HALO + p * tp
            aext_ref[r0:r0 + tp, :] = a
            ac = cb_ref[...] + a * cw_ref[CONV_W - 1:CONV_W, :]
            for tap in range(CONV_W - 1):
                back = CONV_W - 1 - tap
                ac = ac + aext_ref[r0 - back:r0 - back + tp, :] * cw_ref[tap:tap + 1, :]
            act_ref[rows, :] = (ac * jax.nn.sigmoid(ac) * b).astype(BF16)
        carry_ref[j] = aext_ref[tm:tm + CONV_HALO, :]


def _ffn_up(h, g, w_up, cw, cb, *, layer, tm, tn, parts):
    m = h.shape[0]
    n_j = D_FF // tn
    rc = tm // NORM_CHUNKS
    assert tm % (BF16_ROWS * parts) == 0
    kern = functools.partial(_ffn_up_kernel, rc=rc, tm=tm, parts=parts)
    h_spec, g_spec = _norm_specs(tm, rc)
    return pl.pallas_call(
        kern,
        grid=(m // tm, NORM_CHUNKS + n_j),
        in_specs=[
            h_spec,
            g_spec,
            pl.BlockSpec((None, D_MODEL, tn), lambda i, jj: (layer, 0, _col(jj))),
            pl.BlockSpec((None, D_MODEL, tn), lambda i, jj: (layer, 0, n_j + _col(jj))),
            pl.BlockSpec((CONV_W, tn), lambda i, jj: (0, _col(jj))),
            pl.BlockSpec((1, tn), lambda i, jj: (0, _col(jj))),
        ],
        out_specs=pl.BlockSpec((tm, tn), lambda i, jj: (i, _col(jj))),
        out_shape=jax.ShapeDtypeStruct((m, D_FF), BF16),
        scratch_shapes=[
            pltpu.VMEM((tm, D_MODEL), BF16),
            pltpu.VMEM((CONV_HALO + tm, tn), F32),
            pltpu.VMEM((n_j, CONV_HALO, tn), F32),
        ],
        compiler_params=_params(("arbitrary", "arbitrary")),
        name="ffn_up",
    )(h, g, w_up, w_up, cw, cb)


def _final_norm_kernel(h_ref, g_ref, out_ref):
    out_ref[0] = _rms_rows(h_ref[...], g_ref[...])


def _final_norm(h, g, *, batch, seq, lp, tr):
    skip = (META_PAD + N_META) // tr
    per_b = lp // tr
    return pl.pallas_call(
        _final_norm_kernel,
        grid=(batch, seq // tr),
        in_specs=[
            pl.BlockSpec((tr, D_MODEL), lambda b, c: (b * per_b + skip + c, 0)),
            pl.BlockSpec((1, D_MODEL), lambda b, c: (0, 0)),
        ],
        out_specs=pl.BlockSpec((1, tr, D_MODEL), lambda b, c: (b, c, 0)),
        out_shape=jax.ShapeDtypeStruct((batch, seq, D_MODEL), F32),
        compiler_params=_params(("arbitrary", "arbitrary")),
        name="final_norm",
    )(h, g)


def _decay_tables(chunk):
    log_gamma = jnp.log1p(-jnp.exp2(-5.0 - jnp.arange(RET_HEADS, dtype=F32)))
    idx = jnp.arange(chunk, dtype=F32)
    rel = idx[:, None] - idx[None, :]
    dmask = jnp.where(rel[None] >= 0, jnp.exp(log_gamma[:, None, None] * jnp.maximum(rel, 0.0)[None]), 0.0)
    qd = jnp.exp(log_gamma[:, None] * (idx + 1.0)[None, :])[:, :, None]
    kd = jnp.exp(log_gamma[:, None] * (chunk - 1.0 - idx)[None, :])[:, :, None]
    cd = jnp.exp(log_gamma * chunk)[:, None, None]
    return dmask, qd, kd, cd


def _pool_band(chunk):
    r = jnp.arange(chunk)[:, None] + POOL_HALO
    c = jnp.arange(POOL_HALO + chunk)[None, :]
    lag = r - c
    return jnp.stack([jnp.logical_and(lag >= 0, lag < w) for w in POOL_WINDOWS]).astype(BF16)


def kernel(x, meta_tokens, norm1_g, w_in, pool_w, pool_scale, w_out, norm2_g, w_up, conv_w, conv_b, w_down, final_g):
    batch, seq, d = x.shape
    depth = w_in.shape[0]
    assert d == D_MODEL and meta_tokens.shape == (N_META, D_MODEL)
    assert w_up.shape[2] == 2 * D_FF and w_down.shape[1] == D_FF
    lp = META_PAD + N_META + seq
    m = batch * lp
    tm_norm = m // 4
    tm_out = m // 3
    tm_down = m // 4
    chunk = 384
    assert lp % chunk == 0
    assert tm_norm % (BF16_ROWS * NORM_CHUNKS) == 0
    for t in (tm_norm, tm_out, tm_down):
        assert m % t == 0

    meta = jnp.broadcast_to(meta_tokens[None].astype(x.dtype), (batch, N_META, d))
    h = jnp.concatenate([jnp.zeros((batch, META_PAD, d), x.dtype), meta, x], axis=1).reshape(m, d)

    pos = (jnp.arange(lp) - META_PAD).astype(F32)
    inv = jnp.power(ROPE_BASE, -jnp.arange(HALF, dtype=F32) / HALF)
    ang = pos[:, None] * inv[None, :]
    cos = jnp.tile(jnp.cos(ang), (batch, 1))
    sin = jnp.tile(jnp.sin(ang), (batch, 1))
    dmask, qd, kd, cd = _decay_tables(chunk)
    band = _pool_band(chunk)

    for l in range(depth):
        proj = _in_proj(h, norm1_g[l][None], w_in, cos, sin, layer=l, tm=tm_norm, tn=512, parts=(2, 3)[l % 2])
        mixed = _mixer(proj, dmask, qd, kd, cd, band, pool_w[l].astype(BF16), pool_scale[l][None],
                       batch=batch, lp=lp, chunk=chunk)
        h = _proj_res(mixed, w_out, h, layer=l, tm=tm_out, tn=256, parts=4, name="out_proj")
        act = _ffn_up(h, norm2_g[l][None], w_up, conv_w[l], conv_b[l][None], layer=l, tm=tm_norm, tn=256,
                      parts=(3, 4)[l % 2])
        for kb in range(2):
            h = _proj_res(act, w_down, h, layer=l, tm=tm_down, tn=256, parts=4, name="ffn_down",
                          k_blocks=2, k_block=kb)

    return _final_norm(h, final_g[None], batch=batch, seq=seq, lp=lp, tr=ROW_ALIGN)
```

```python
import functools

import jax
import jax.numpy as jnp
from jax import lax
from jax.experimental import pallas as pl
from jax.experimental.pallas import tpu as pltpu

D_MODEL = 4096
N_META = 16
RET_W = 2048
POOL_W = 2048
RET_HEADS = 8
HEAD_DIM = 256
HALF = HEAD_DIM // 2
POOL_WINDOWS = (2, 4, 8, 16)
POOL_GROUP = 512
D_FF = 11008
CONV_W = 3
ROPE_BASE = 10000.0
EPS = 1e-6

LANE = 128
BF16_ROWS = 16
ROW_ALIGN = 128
META_PAD = (-N_META) % ROW_ALIGN
POOL_HALO = 128
CONV_HALO = 8
VMEM_LIMIT = 58 * 1024 * 1024

BF16 = jnp.bfloat16
F32 = jnp.float32


def _params(sem):
    return pltpu.CompilerParams(dimension_semantics=sem, vmem_limit_bytes=VMEM_LIMIT)


def _rms_rows(x, g):
    return x * lax.rsqrt(jnp.mean(x * x, axis=-1, keepdims=True) + EPS) * g


def _norm_phase(jj, h_ref, g_ref, hn_ref, nc):
    rc = hn_ref.shape[0] // nc

    @pl.when(jj < nc)
    def _():
        r0 = pl.multiple_of(jj * rc, BF16_ROWS)
        hn_ref[pl.ds(r0, rc), :] = _rms_rows(h_ref[...], g_ref[...]).astype(BF16)


def _norm_specs(tm, nc):
    assert tm % (BF16_ROWS * nc) == 0
    h_spec = pl.BlockSpec((tm // nc, D_MODEL), lambda i, jj: (i * nc + jnp.minimum(jj, nc - 1), 0))
    g_spec = pl.BlockSpec((1, D_MODEL), lambda i, jj: (0, 0))
    return h_spec, g_spec


def _col(jj, nc):
    return jnp.maximum(jj - nc, 0)


def _in_proj_kernel(h_ref, g_ref, w_ref, cos_ref, sin_ref, out_ref, hn_ref, *, nc, tn, n_q, n_rot, parts):
    jj = pl.program_id(1)
    _norm_phase(jj, h_ref, g_ref, hn_ref, nc)

    @pl.when(jj >= nc)
    def _():
        j = jj - nc
        w = w_ref[...].astype(BF16)
        is_rot = j < n_rot
        scale = jnp.where(jnp.logical_and(j >= n_q, is_rot), HEAD_DIM ** -0.5, 1.0).astype(F32)
        tp = hn_ref.shape[0] // parts
        for p in range(parts):
            rows = slice(p * tp, (p + 1) * tp)
            acc = jnp.dot(hn_ref[rows, :], w, preferred_element_type=F32)
            cos = jnp.where(is_rot, cos_ref[rows, :], 1.0) * scale
            sin = jnp.where(is_rot, sin_ref[rows, :], 0.0) * scale
            for hh in range(tn // HEAD_DIM):
                x1 = acc[:, hh * HEAD_DIM: hh * HEAD_DIM + HALF]
                x2 = acc[:, hh * HEAD_DIM + HALF: (hh + 1) * HEAD_DIM]
                out_ref[rows, hh * HEAD_DIM: hh * HEAD_DIM + HALF] = (x1 * cos - x2 * sin).astype(BF16)
                out_ref[rows, hh * HEAD_DIM + HALF: (hh + 1) * HEAD_DIM] = (x1 * sin + x2 * cos).astype(BF16)


def _in_proj(h, g, w, cos, sin, *, layer, tm, tn, nc, parts):
    m = h.shape[0]
    n = w.shape[2]
    assert tm % (BF16_ROWS * parts) == 0
    kern = functools.partial(_in_proj_kernel, nc=nc, tn=tn, n_q=RET_W // tn, n_rot=2 * RET_W // tn, parts=parts)
    h_spec, g_spec = _norm_specs(tm, nc)
    return pl.pallas_call(
        kern,
        grid=(m // tm, nc + n // tn),
        in_specs=[
            h_spec,
            g_spec,
            pl.BlockSpec((None, D_MODEL, tn), lambda i, jj: (layer, 0, _col(jj, nc))),
            pl.BlockSpec((tm, HALF), lambda i, jj: (i, 0)),
            pl.BlockSpec((tm, HALF), lambda i, jj: (i, 0)),
        ],
        out_specs=pl.BlockSpec((tm, tn), lambda i, jj: (i, _col(jj, nc))),
        out_shape=jax.ShapeDtypeStruct((m, n), BF16),
        scratch_shapes=[pltpu.VMEM((tm, D_MODEL), BF16)],
        compiler_params=_params(("arbitrary", "arbitrary")),
        name="in_proj",
    )(h, g, w, cos, sin)


def _mixer_kernel(q_ref, k_ref, v_ref, g_ref, p_ref, dmask_ref, qd_ref, kd_ref, cd_ref, band_ref, pw_ref, ps_ref,
                  out_ref, state_ref, pext_ref, *, chunk):
    c = pl.program_id(1)

    @pl.when(c == 0)
    def _():
        state_ref[...] = jnp.zeros_like(state_ref)
        pext_ref[0:POOL_HALO, :] = jnp.zeros((POOL_HALO, POOL_W), BF16)

    for hh in range(RET_HEADS):
        sl = slice(hh * HEAD_DIM, (hh + 1) * HEAD_DIM)
        q = q_ref[:, sl]
        k = k_ref[:, sl]
        v = v_ref[:, sl]
        scores = lax.dot_general(q, k, (((1,), (1,)), ((), ())), preferred_element_type=F32)
        scores = (scores * dmask_ref[hh]).astype(BF16)
        intra = jnp.dot(scores, v, preferred_element_type=F32)
        state = state_ref[hh]
        cross = jnp.dot(q, state.astype(BF16), preferred_element_type=F32) * qd_ref[hh]
        kdec = (k.astype(F32) * kd_ref[hh]).astype(BF16)
        upd = lax.dot_general(kdec, v, (((0,), (0,)), ((), ())), preferred_element_type=F32)
        state_ref[hh] = state * cd_ref[hh] + upd
        r = intra + cross
        r = r * lax.rsqrt(jnp.mean(r * r, axis=-1, keepdims=True) + EPS)
        gate = g_ref[:, sl].astype(F32)
        out_ref[:, sl] = (r * (gate * jax.nn.sigmoid(gate))).astype(BF16)

    pext_ref[POOL_HALO:POOL_HALO + chunk, :] = p_ref[...]
    t = c * chunk - META_PAD + lax.broadcasted_iota(jnp.int32, (chunk, 1), 0)
    valid = t >= 0
    for gi, w in enumerate(POOL_WINDOWS):
        cs = slice(gi * POOL_GROUP, (gi + 1) * POOL_GROUP)
        xg = p_ref[:, cs].astype(F32)
        win = jnp.dot(band_ref[gi], pext_ref[:, cs], preferred_element_type=F32)
        cnt = jnp.clip(t + 1, 1, w).astype(F32)
        mix = jnp.where(valid, win / cnt - xg, 0.0).astype(BF16)
        mg = jnp.dot(mix, pw_ref[gi], preferred_element_type=F32) * ps_ref[:, cs]
        out_ref[:, RET_W + gi * POOL_GROUP: RET_W + (gi + 1) * POOL_GROUP] = mg.astype(BF16)
    pext_ref[0:POOL_HALO, :] = pext_ref[chunk:chunk + POOL_HALO, :]


def _mixer(proj, dmask, qd, kd, cd, band, pool_w, pool_scale, *, batch, lp, chunk):
    nc = lp // chunk
    m = proj.shape[0]
    assert chunk >= POOL_HALO
    kern = functools.partial(_mixer_kernel, chunk=chunk)
    const3 = lambda b, c: (0, 0, 0)
    col_block = lambda n: pl.BlockSpec((chunk, RET_W), lambda b, c: (b * nc + c, n))
    return pl.pallas_call(
        kern,
        grid=(batch, nc),
        in_specs=[
            col_block(0), col_block(1), col_block(2), col_block(3), col_block(4),
            pl.BlockSpec((RET_HEADS, chunk, chunk), const3),
            pl.BlockSpec((RET_HEADS, chunk, 1), const3),
            pl.BlockSpec((RET_HEADS, chunk, 1), const3),
            pl.BlockSpec((RET_HEADS, 1, 1), const3),
            pl.BlockSpec((len(POOL_WINDOWS), chunk, POOL_HALO + chunk), const3),
            pl.BlockSpec((len(POOL_WINDOWS), POOL_GROUP, POOL_GROUP), const3),
            pl.BlockSpec((1, POOL_W), lambda b, c: (0, 0)),
        ],
        out_specs=pl.BlockSpec((chunk, RET_W + POOL_W), lambda b, c: (b * nc + c, 0)),
        out_shape=jax.ShapeDtypeStruct((m, RET_W + POOL_W), BF16),
        scratch_shapes=[
            pltpu.VMEM((RET_HEADS, HEAD_DIM, HEAD_DIM), F32),
            pltpu.VMEM((POOL_HALO + chunk, POOL_W), BF16),
        ],
        compiler_params=_params(("arbitrary", "arbitrary")),
        name="mixer",
    )(proj, proj, proj, proj, proj, dmask, qd, kd, cd, band, pool_w, pool_scale)


def _proj_res_kernel(lhs_ref, w_ref, res_ref, out_ref, *, parts):
    w = w_ref[...].astype(BF16)
    tp = lhs_ref.shape[0] // parts
    for p in range(parts):
        rows = slice(p * tp, (p + 1) * tp)
        out_ref[rows, :] = res_ref[rows, :] + jnp.dot(lhs_ref[rows, :], w, preferred_element_type=F32)


def _proj_res(lhs, w, res, *, layer, tm, tn, parts, name, k_blocks=1, k_block=0):
    m, k = lhs.shape
    n = w.shape[2]
    kb = k // k_blocks
    assert tm % (BF16_ROWS * parts) == 0 and k % k_blocks == 0 and kb % LANE == 0
    return pl.pallas_call(
        functools.partial(_proj_res_kernel, parts=parts),
        grid=(m // tm, n // tn),
        in_specs=[
            pl.BlockSpec((tm, kb), lambda i, j: (i, k_block), pipeline_mode=pl.Buffered(1)),
            pl.BlockSpec((None, kb, tn), lambda i, j: (layer, k_block, j)),
            pl.BlockSpec((tm, tn), lambda i, j: (i, j)),
        ],
        out_specs=pl.BlockSpec((tm, tn), lambda i, j: (i, j)),
        out_shape=jax.ShapeDtypeStruct((m, n), F32),
        compiler_params=_params(("arbitrary", "arbitrary")),
        name=name,
    )(lhs, w, res)


def _ffn_up_kernel(h_ref, g_ref, wa_ref, wb_ref, cw_ref, cb_ref, act_ref, hn_ref, aext_ref, carry_ref,
                   *, nc, tm, parts):
    i = pl.program_id(0)
    jj = pl.program_id(1)
    _norm_phase(jj, h_ref, g_ref, hn_ref, nc)

    @pl.when(jj >= nc)
    def _():
        j = jj - nc
        @pl.when(i == 0)
        def _():
            carry_ref[j] = jnp.zeros(carry_ref.shape[1:], F32)

        wa = wa_ref[...].astype(BF16)
        wb = wb_ref[...].astype(BF16)
        aext_ref[0:CONV_HALO, :] = carry_ref[j]
        tp = tm // parts
        for p in range(parts):
            rows = slice(p * tp, (p + 1) * tp)
            hn = hn_ref[rows, :]
            a = jnp.dot(hn, wa, preferred_element_type=F32)
            b = jnp.dot(hn, wb, preferred_element_type=F32)
            r0 = CONV_HALO + p * tp
            aext_ref[r0:r0 + tp, :] = a
            ac = cb_ref[...] + a * cw_ref[CONV_W - 1:CONV_W, :]
            for tap in range(CONV_W - 1):
                back = CONV_W - 1 - tap
                ac = ac + aext_ref[r0 - back:r0 - back + tp, :] * cw_ref[tap:tap + 1, :]
            act_ref[rows, :] = (ac * jax.nn.sigmoid(ac) * b).astype(BF16)
        carry_ref[j] = aext_ref[tm:tm + CONV_HALO, :]


def _ffn_up(h, g, w_up, cw, cb, *, layer, tm, tn, nc, parts):
    m = h.shape[0]
    n_j = D_FF // tn
    assert tm % (BF16_ROWS * parts) == 0
    kern = functools.partial(_ffn_up_kernel, nc=nc, tm=tm, parts=parts)
    h_spec, g_spec = _norm_specs(tm, nc)
    return pl.pallas_call(
        kern,
        grid=(m // tm, nc + n_j),
        in_specs=[
            h_spec,
            g_spec,
            pl.BlockSpec((None, D_MODEL, tn), lambda i, jj: (layer, 0, _col(jj, nc))),
            pl.BlockSpec((None, D_MODEL, tn), lambda i, jj: (layer, 0, n_j + _col(jj, nc))),
            pl.BlockSpec((CONV_W, tn), lambda i, jj: (0, _col(jj, nc))),
            pl.BlockSpec((1, tn), lambda i, jj: (0, _col(jj, nc))),
        ],
        out_specs=pl.BlockSpec((tm, tn), lambda i, jj: (i, _col(jj, nc))),
        out_shape=jax.ShapeDtypeStruct((m, D_FF), BF16),
        scratch_shapes=[
            pltpu.VMEM((tm, D_MODEL), BF16),
            pltpu.VMEM((CONV_HALO + tm, tn), F32),
            pltpu.VMEM((n_j, CONV_HALO, tn), F32),
        ],
        compiler_params=_params(("arbitrary", "arbitrary")),
        name="ffn_up",
    )(h, g, w_up, w_up, cw, cb)


def _final_norm_kernel(h_ref, g_ref, out_ref):
    out_ref[0] = _rms_rows(h_ref[...], g_ref[...])


def _final_norm(h, g, *, batch, seq, lp, tr):
    skip = (META_PAD + N_META) // tr
    per_b = lp // tr
    return pl.pallas_call(
        _final_norm_kernel,
        grid=(batch, seq // tr),
        in_specs=[
            pl.BlockSpec((tr, D_MODEL), lambda b, c: (b * per_b + skip + c, 0)),
            pl.BlockSpec((1, D_MODEL), lambda b, c: (0, 0)),
        ],
        out_specs=pl.BlockSpec((1, tr, D_MODEL), lambda b, c: (b, c, 0)),
        out_shape=jax.ShapeDtypeStruct((batch, seq, D_MODEL), F32),
        compiler_params=_params(("arbitrary", "arbitrary")),
        name="final_norm",
    )(h, g)


def _decay_tables(chunk):
    log_gamma = jnp.log1p(-jnp.exp2(-5.0 - jnp.arange(RET_HEADS, dtype=F32)))
    idx = jnp.arange(chunk, dtype=F32)
    rel = idx[:, None] - idx[None, :]
    dmask = jnp.where(rel[None] >= 0, jnp.exp(log_gamma[:, None, None] * jnp.maximum(rel, 0.0)[None]), 0.0)
    qd = jnp.exp(log_gamma[:, None] * (idx + 1.0)[None, :])[:, :, None]
    kd = jnp.exp(log_gamma[:, None] * (chunk - 1.0 - idx)[None, :])[:, :, None]
    cd = jnp.exp(log_gamma * chunk)[:, None, None]
    return dmask, qd, kd, cd


def _pool_band(chunk):
    r = jnp.arange(chunk)[:, None] + POOL_HALO
    c = jnp.arange(POOL_HALO + chunk)[None, :]
    lag = r - c
    return jnp.stack([jnp.logical_and(lag >= 0, lag < w) for w in POOL_WINDOWS]).astype(BF16)


def kernel(x, meta_tokens, norm1_g, w_in, pool_w, pool_scale, w_out, norm2_g, w_up, conv_w, conv_b, w_down, final_g):
    batch, seq, d = x.shape
    depth = w_in.shape[0]
    assert d == D_MODEL and meta_tokens.shape == (N_META, D_MODEL)
    assert w_up.shape[2] == 2 * D_FF and w_down.shape[1] == D_FF
    lp = META_PAD + N_META + seq
    m = batch * lp
    tm_norm = m // 4
    tm_out = m // 3
    tm_down = m // 4
    chunk = 384
    assert lp % chunk == 0
    for t in (tm_norm, tm_out, tm_down):
        assert m % t == 0

    meta = jnp.broadcast_to(meta_tokens[None].astype(x.dtype), (batch, N_META, d))
    h = jnp.concatenate([jnp.zeros((batch, META_PAD, d), x.dtype), meta, x], axis=1).reshape(m, d)

    pos = (jnp.arange(lp) - META_PAD).astype(F32)
    inv = jnp.power(ROPE_BASE, -jnp.arange(HALF, dtype=F32) / HALF)
    ang = pos[:, None] * inv[None, :]
    cos = jnp.tile(jnp.cos(ang), (batch, 1))
    sin = jnp.tile(jnp.sin(ang), (batch, 1))
    dmask, qd, kd, cd = _decay_tables(chunk)
    band = _pool_band(chunk)

    for l in range(depth):
        ab = l % 2
        proj = _in_proj(h, norm1_g[l][None], w_in, cos, sin, layer=l, tm=tm_norm, tn=512, nc=(12, 6)[ab], parts=3)
        mixed = _mixer(proj, dmask, qd, kd, cd, band, pool_w[l].astype(BF16), pool_scale[l][None],
                       batch=batch, lp=lp, chunk=chunk)
        h = _proj_res(mixed, w_out, h, layer=l, tm=tm_out, tn=256, parts=(4, 8)[ab], name="out_proj")
        act = _ffn_up(h, norm2_g[l][None], w_up, conv_w[l], conv_b[l][None], layer=l, tm=tm_norm, tn=256,
                      nc=(12, 6)[ab], parts=4)
        for kb in range(2):
            h = _proj_res(act, w_down, h, layer=l, tm=tm_down, tn=256, parts=(4, 6)[ab], name="ffn_down",
                          k_blocks=2, k_block=kb)

    return _final_norm(h, final_g[None], batch=batch, seq=seq, lp=lp, tr=ROW_ALIGN)
```

```python
import functools

import jax
import jax.numpy as jnp
from jax import lax
from jax.experimental import pallas as pl
from jax.experimental.pallas import tpu as pltpu

D_MODEL = 4096
N_META = 16
RET_W = 2048
POOL_W = 2048
RET_HEADS = 8
HEAD_DIM = 256
HALF = HEAD_DIM // 2
POOL_WINDOWS = (2, 4, 8, 16)
POOL_GROUP = 512
D_FF = 11008
CONV_W = 3
ROPE_BASE = 10000.0
EPS = 1e-6

LANE = 128
BF16_ROWS = 16
ROW_ALIGN = 128
META_PAD = (-N_META) % ROW_ALIGN
POOL_HALO = 128
CONV_HALO = 8
VMEM_LIMIT = 58 * 1024 * 1024

BF16 = jnp.bfloat16
F32 = jnp.float32


def _params(sem):
    return pltpu.CompilerParams(dimension_semantics=sem, vmem_limit_bytes=VMEM_LIMIT)


def _rms_rows(x, g):
    return x * lax.rsqrt(jnp.mean(x * x, axis=-1, keepdims=True) + EPS) * g


def _norm_phase(jj, h_ref, g_ref, hn_ref, nc):
    rc = hn_ref.shape[0] // nc

    @pl.when(jj < nc)
    def _():
        r0 = pl.multiple_of(jj * rc, BF16_ROWS)
        hn_ref[pl.ds(r0, rc), :] = _rms_rows(h_ref[...], g_ref[...]).astype(BF16)


def _norm_specs(tm, nc):
    assert tm % (BF16_ROWS * nc) == 0
    h_spec = pl.BlockSpec((tm // nc, D_MODEL), lambda i, jj: (i * nc + jnp.minimum(jj, nc - 1), 0))
    g_spec = pl.BlockSpec((1, D_MODEL), lambda i, jj: (0, 0))
    return h_spec, g_spec


def _col(jj, nc):
    return jnp.maximum(jj - nc, 0)


def _in_proj_kernel(h_ref, g_ref, w_ref, cos_ref, sin_ref, out_ref, hn_ref, *, nc, tn, n_q, n_rot, parts):
    jj = pl.program_id(1)
    _norm_phase(jj, h_ref, g_ref, hn_ref, nc)

    @pl.when(jj >= nc)
    def _():
        j = jj - nc
        w = w_ref[...].astype(BF16)
        is_rot = j < n_rot
        scale = jnp.where(jnp.logical_and(j >= n_q, is_rot), HEAD_DIM ** -0.5, 1.0).astype(F32)
        tp = hn_ref.shape[0] // parts
        for p in range(parts):
            rows = slice(p * tp, (p + 1) * tp)
            acc = jnp.dot(hn_ref[rows, :], w, preferred_element_type=F32)
            cos = jnp.where(is_rot, cos_ref[rows, :], 1.0) * scale
            sin = jnp.where(is_rot, sin_ref[rows, :], 0.0) * scale
            for hh in range(tn // HEAD_DIM):
                x1 = acc[:, hh * HEAD_DIM: hh * HEAD_DIM + HALF]
                x2 = acc[:, hh * HEAD_DIM + HALF: (hh + 1) * HEAD_DIM]
                out_ref[rows, hh * HEAD_DIM: hh * HEAD_DIM + HALF] = (x1 * cos - x2 * sin).astype(BF16)
                out_ref[rows, hh * HEAD_DIM + HALF: (hh + 1) * HEAD_DIM] = (x1 * sin + x2 * cos).astype(BF16)


def _in_proj(h, g, w, cos, sin, *, layer, tm, tn, nc, parts):
    m = h.shape[0]
    n = w.shape[2]
    assert tm % (BF16_ROWS * parts) == 0
    kern = functools.partial(_in_proj_kernel, nc=nc, tn=tn, n_q=RET_W // tn, n_rot=2 * RET_W // tn, parts=parts)
    h_spec, g_spec = _norm_specs(tm, nc)
    return pl.pallas_call(
        kern,
        grid=(m // tm, nc + n // tn),
        in_specs=[
            h_spec,
            g_spec,
            pl.BlockSpec((None, D_MODEL, tn), lambda i, jj: (layer, 0, _col(jj, nc))),
            pl.BlockSpec((tm, HALF), lambda i, jj: (i, 0)),
            pl.BlockSpec((tm, HALF), lambda i, jj: (i, 0)),
        ],
        out_specs=pl.BlockSpec((tm, tn), lambda i, jj: (i, _col(jj, nc))),
        out_shape=jax.ShapeDtypeStruct((m, n), BF16),
        scratch_shapes=[pltpu.VMEM((tm, D_MODEL), BF16)],
        compiler_params=_params(("arbitrary", "arbitrary")),
        name="in_proj",
    )(h, g, w, cos, sin)


def _mixer_kernel(q_ref, k_ref, v_ref, g_ref, p_ref, dmask_ref, qd_ref, kd_ref, cd_ref, band_ref, pw_ref, ps_ref,
                  out_ref, state_ref, pext_ref, *, chunk):
    c = pl.program_id(1)

    @pl.when(c == 0)
    def _():
        state_ref[...] = jnp.zeros_like(state_ref)
        pext_ref[0:POOL_HALO, :] = jnp.zeros((POOL_HALO, POOL_W), BF16)

    for hh in range(RET_HEADS):
        sl = slice(hh * HEAD_DIM, (hh + 1) * HEAD_DIM)
        q = q_ref[:, sl]
        k = k_ref[:, sl]
        v = v_ref[:, sl]
        scores = lax.dot_general(q, k, (((1,), (1,)), ((), ())), preferred_element_type=F32)
        scores = (scores * dmask_ref[hh]).astype(BF16)
        intra = jnp.dot(scores, v, preferred_element_type=F32)
        state = state_ref[hh]
        cross = jnp.dot(q, state.astype(BF16), preferred_element_type=F32) * qd_ref[hh]
        kdec = (k.astype(F32) * kd_ref[hh]).astype(BF16)
        upd = lax.dot_general(kdec, v, (((0,), (0,)), ((), ())), preferred_element_type=F32)
        state_ref[hh] = state * cd_ref[hh] + upd
        r = intra + cross
        r = r * lax.rsqrt(jnp.mean(r * r, axis=-1, keepdims=True) + EPS)
        gate = g_ref[:, sl].astype(F32)
        out_ref[:, sl] = (r * (gate * jax.nn.sigmoid(gate))).astype(BF16)

    pext_ref[POOL_HALO:POOL_HALO + chunk, :] = p_ref[...]
    t = c * chunk - META_PAD + lax.broadcasted_iota(jnp.int32, (chunk, 1), 0)
    valid = t >= 0
    for gi, w in enumerate(POOL_WINDOWS):
        cs = slice(gi * POOL_GROUP, (gi + 1) * POOL_GROUP)
        xg = p_ref[:, cs].astype(F32)
        win = jnp.dot(band_ref[gi], pext_ref[:, cs], preferred_element_type=F32)
        cnt = jnp.clip(t + 1, 1, w).astype(F32)
        mix = jnp.where(valid, win / cnt - xg, 0.0).astype(BF16)
        mg = jnp.dot(mix, pw_ref[gi], preferred_element_type=F32) * ps_ref[:, cs]
        out_ref[:, RET_W + gi * POOL_GROUP: RET_W + (gi + 1) * POOL_GROUP] = mg.astype(BF16)
    pext_ref[0:POOL_HALO, :] = pext_ref[chunk:chunk + POOL_HALO, :]


def _mixer(proj, dmask, qd, kd, cd, band, pool_w, pool_scale, *, batch, lp, chunk):
    nc = lp // chunk
    m = proj.shape[0]
    assert chunk >= POOL_HALO
    kern = functools.partial(_mixer_kernel, chunk=chunk)
    const3 = lambda b, c: (0, 0, 0)
    col_block = lambda n: pl.BlockSpec((chunk, RET_W), lambda b, c: (b * nc + c, n))
    return pl.pallas_call(
        kern,
        grid=(batch, nc),
        in_specs=[
            col_block(0), col_block(1), col_block(2), col_block(3), col_block(4),
            pl.BlockSpec((RET_HEADS, chunk, chunk), const3),
            pl.BlockSpec((RET_HEADS, chunk, 1), const3),
            pl.BlockSpec((RET_HEADS, chunk, 1), const3),
            pl.BlockSpec((RET_HEADS, 1, 1), const3),
            pl.BlockSpec((len(POOL_WINDOWS), chunk, POOL_HALO + chunk), const3),
            pl.BlockSpec((len(POOL_WINDOWS), POOL_GROUP, POOL_GROUP), const3),
            pl.BlockSpec((1, POOL_W), lambda b, c: (0, 0)),
        ],
        out_specs=pl.BlockSpec((chunk, RET_W + POOL_W), lambda b, c: (b * nc + c, 0)),
        out_shape=jax.ShapeDtypeStruct((m, RET_W + POOL_W), BF16),
        scratch_shapes=[
            pltpu.VMEM((RET_HEADS, HEAD_DIM, HEAD_DIM), F32),
            pltpu.VMEM((POOL_HALO + chunk, POOL_W), BF16),
        ],
        compiler_params=_params(("arbitrary", "arbitrary")),
        name="mixer",
    )(proj, proj, proj, proj, proj, dmask, qd, kd, cd, band, pool_w, pool_scale)


def _proj_res_kernel(lhs_ref, w_ref, res_ref, out_ref, *, parts):
    w = w_ref[...].astype(BF16)
    tp = lhs_ref.shape[0] // parts
    for p in range(parts):
        rows = slice(p * tp, (p + 1) * tp)
        out_ref[rows, :] = res_ref[rows, :] + jnp.dot(lhs_ref[rows, :], w, preferred_element_type=F32)


def _proj_res(lhs, w, res, *, layer, tm, tn, parts, name, k_blocks=1, k_block=0):
    m, k = lhs.shape
    n = w.shape[2]
    kb = k // k_blocks
    assert tm % (BF16_ROWS * parts) == 0 and k % k_blocks == 0 and kb % LANE == 0
    return pl.pallas_call(
        functools.partial(_proj_res_kernel, parts=parts),
        grid=(m // tm, n // tn),
        in_specs=[
            pl.BlockSpec((tm, kb), lambda i, j: (i, k_block), pipeline_mode=pl.Buffered(1)),
            pl.BlockSpec((None, kb, tn), lambda i, j: (layer, k_block, j)),
            pl.BlockSpec((tm, tn), lambda i, j: (i, j)),
        ],
        out_specs=pl.BlockSpec((tm, tn), lambda i, j: (i, j)),
        out_shape=jax.ShapeDtypeStruct((m, n), F32),
        compiler_params=_params(("arbitrary", "arbitrary")),
        name=name,
    )(lhs, w, res)


def _ffn_up_kernel(h_ref, g_ref, wa_ref, wb_ref, conv_ref, act_ref, hn_ref, aext_ref, carry_ref,
                   *, nc, tm, parts):
    i = pl.program_id(0)
    jj = pl.program_id(1)
    _norm_phase(jj, h_ref, g_ref, hn_ref, nc)

    @pl.when(jj >= nc)
    def _():
        j = jj - nc
        @pl.when(i == 0)
        def _():
            carry_ref[j] = jnp.zeros(carry_ref.shape[1:], F32)

        wa = wa_ref[...].astype(BF16)
        wb = wb_ref[...].astype(BF16)
        conv = conv_ref[j]
        cb = conv[CONV_W:CONV_W + 1, :]
        aext_ref[0:CONV_HALO, :] = carry_ref[j]
        tp = tm // parts
        for p in range(parts):
            rows = slice(p * tp, (p + 1) * tp)
            hn = hn_ref[rows, :]
            a = jnp.dot(hn, wa, preferred_element_type=F32)
            b = jnp.dot(hn, wb, preferred_element_type=F32)
            r0 = CONV_HALO + p * tp
            aext_ref[r0:r0 + tp, :] = a
            ac = cb + a * conv[CONV_W - 1:CONV_W, :]
            for tap in range(CONV_W - 1):
                back = CONV_W - 1 - tap
                ac = ac + aext_ref[r0 - back:r0 - back + tp, :] * conv[tap:tap + 1, :]
            act_ref[rows, :] = (ac * jax.nn.sigmoid(ac) * b).astype(BF16)
        carry_ref[j] = aext_ref[tm:tm + CONV_HALO, :]


def _ffn_up(h, g, w_up, conv_w, conv_b, *, layer, tm, tn, nc, parts):
    m = h.shape[0]
    n_j = D_FF // tn
    assert tm % (BF16_ROWS * parts) == 0
    conv = jnp.concatenate([conv_w, conv_b[None]], axis=0).reshape(CONV_W + 1, n_j, tn).transpose(1, 0, 2)
    kern = functools.partial(_ffn_up_kernel, nc=nc, tm=tm, parts=parts)
    h_spec, g_spec = _norm_specs(tm, nc)
    return pl.pallas_call(
        kern,
        grid=(m // tm, nc + n_j),
        in_specs=[
            h_spec,
            g_spec,
            pl.BlockSpec((None, D_MODEL, tn), lambda i, jj: (layer, 0, _col(jj, nc))),
            pl.BlockSpec((None, D_MODEL, tn), lambda i, jj: (layer, 0, n_j + _col(jj, nc))),
            pl.BlockSpec((n_j, CONV_W + 1, tn), lambda i, jj: (0, 0, 0)),
        ],
        out_specs=pl.BlockSpec((tm, tn), lambda i, jj: (i, _col(jj, nc))),
        out_shape=jax.ShapeDtypeStruct((m, D_FF), BF16),
        scratch_shapes=[
            pltpu.VMEM((tm, D_MODEL), BF16),
            pltpu.VMEM((CONV_HALO + tm, tn), F32),
            pltpu.VMEM((n_j, CONV_HALO, tn), F32),
        ],
        compiler_params=_params(("arbitrary", "arbitrary")),
        name="ffn_up",
    )(h, g, w_up, w_up, conv)


def _final_norm_kernel(h_ref, g_ref, out_ref):
    out_ref[0] = _rms_rows(h_ref[...], g_ref[...])


def _final_norm(h, g, *, batch, seq, lp, tr):
    skip = (META_PAD + N_META) // tr
    per_b = lp // tr
    return pl.pallas_call(
        _final_norm_kernel,
        grid=(batch, seq // tr),
        in_specs=[
            pl.BlockSpec((tr, D_MODEL), lambda b, c: (b * per_b + skip + c, 0)),
            pl.BlockSpec((1, D_MODEL), lambda b, c: (0, 0)),
        ],
        out_specs=pl.BlockSpec((1, tr, D_MODEL), lambda b, c: (b, c, 0)),
        out_shape=jax.ShapeDtypeStruct((batch, seq, D_MODEL), F32),
        compiler_params=_params(("arbitrary", "arbitrary")),
        name="final_norm",
    )(h, g)


def _decay_tables(chunk):
    log_gamma = jnp.log1p(-jnp.exp2(-5.0 - jnp.arange(RET_HEADS, dtype=F32)))
    idx = jnp.arange(chunk, dtype=F32)
    rel = idx[:, None] - idx[None, :]
    dmask = jnp.where(rel[None] >= 0, jnp.exp(log_gamma[:, None, None] * jnp.maximum(rel, 0.0)[None]), 0.0)
    qd = jnp.exp(log_gamma[:, None] * (idx + 1.0)[None, :])[:, :, None]
    kd = jnp.exp(log_gamma[:, None] * (chunk - 1.0 - idx)[None, :])[:, :, None]
    cd = jnp.exp(log_gamma * chunk)[:, None, None]
    return dmask, qd, kd, cd


def _pool_band(chunk):
    r = jnp.arange(chunk)[:, None] + POOL_HALO
    c = jnp.arange(POOL_HALO + chunk)[None, :]
    lag = r - c
    return jnp.stack([jnp.logical_and(lag >= 0, lag < w) for w in POOL_WINDOWS]).astype(BF16)


def kernel(x, meta_tokens, norm1_g, w_in, pool_w, pool_scale, w_out, norm2_g, w_up, conv_w, conv_b, w_down, final_g):
    batch, seq, d = x.shape
    depth = w_in.shape[0]
    assert d == D_MODEL and meta_tokens.shape == (N_META, D_MODEL)
    assert w_up.shape[2] == 2 * D_FF and w_down.shape[1] == D_FF
    lp = META_PAD + N_META + seq
    m = batch * lp
    tm_norm = m // 4
    tm_out = m // 3
    tm_down = m // 4
    chunk = 384
    assert lp % chunk == 0
    for t in (tm_norm, tm_out, tm_down):
        assert m % t == 0

    meta = jnp.broadcast_to(meta_tokens[None].astype(x.dtype), (batch, N_META, d))
    h = jnp.concatenate([jnp.zeros((batch, META_PAD, d), x.dtype), meta, x], axis=1).reshape(m, d)

    pos = (jnp.arange(lp) - META_PAD).astype(F32)
    inv = jnp.power(ROPE_BASE, -jnp.arange(HALF, dtype=F32) / HALF)
    ang = pos[:, None] * inv[None, :]
    cos = jnp.tile(jnp.cos(ang), (batch, 1))
    sin = jnp.tile(jnp.sin(ang), (batch, 1))
    dmask, qd, kd, cd = _decay_tables(chunk)
    band = _pool_band(chunk)

    for l in range(depth):
        proj = _in_proj(h, norm1_g[l][None], w_in, cos, sin, layer=l, tm=tm_norm, tn=512, nc=6, parts=3)
        mixed = _mixer(proj, dmask, qd, kd, cd, band, pool_w[l].astype(BF16), pool_scale[l][None],
                       batch=batch, lp=lp, chunk=chunk)
        h = _proj_res(mixed, w_out, h, layer=l, tm=tm_out, tn=256, parts=4, name="out_proj")
        act = _ffn_up(h, norm2_g[l][None], w_up, conv_w[l], conv_b[l], layer=l, tm=tm_norm, tn=256, nc=6, parts=4)
        for kb in range(2):
            h = _proj_res(act, w_down, h, layer=l, tm=tm_down, tn=256, parts=4, name="ffn_down",
                          k_blocks=2, k_block=kb)

    return _final_norm(h, final_g[None], batch=batch, seq=seq, lp=lp, tr=ROW_ALIGN)
```

```python
import functools

import jax
import jax.numpy as jnp
from jax import lax
from jax.experimental import pallas as pl
from jax.experimental.pallas import tpu as pltpu

D_MODEL = 4096
N_META = 16
RET_W = 2048
POOL_W = 2048
RET_HEADS = 8
HEAD_DIM = 256
HALF = HEAD_DIM // 2
POOL_WINDOWS = (2, 4, 8, 16)
POOL_GROUP = 512
D_FF = 11008
CONV_W = 3
ROPE_BASE = 10000.0
EPS = 1e-6

LANE = 128
BF16_ROWS = 16
ROW_ALIGN = 128
META_PAD = (-N_META) % ROW_ALIGN
POOL_HALO = 128
CONV_HALO = 8
VMEM_LIMIT = 58 * 1024 * 1024

BF16 = jnp.bfloat16
F32 = jnp.float32


def _params(sem):
    return pltpu.CompilerParams(dimension_semantics=sem, vmem_limit_bytes=VMEM_LIMIT)


def _rms_rows(x, g):
    return x * lax.rsqrt(jnp.mean(x * x, axis=-1, keepdims=True) + EPS) * g


def _norm_phase(jj, h_ref, g_ref, hn_ref, nc):
    rc = hn_ref.shape[0] // nc

    @pl.when(jj < nc)
    def _():
        r0 = pl.multiple_of(jj * rc, BF16_ROWS)
        hn_ref[pl.ds(r0, rc), :] = _rms_rows(h_ref[...], g_ref[...]).astype(BF16)


def _norm_specs(tm, nc):
    assert tm % (BF16_ROWS * nc) == 0
    h_spec = pl.BlockSpec((tm // nc, D_MODEL), lambda i, jj: (i * nc + jnp.minimum(jj, nc - 1), 0))
    g_spec = pl.BlockSpec((1, D_MODEL), lambda i, jj: (0, 0))
    return h_spec, g_spec


def _col(jj, nc):
    return jnp.maximum(jj - nc, 0)


def _in_proj_kernel(h_ref, g_ref, w_ref, cos_ref, sin_ref, out_ref, hn_ref, *, nc, tn, n_q, n_rot, parts):
    jj = pl.program_id(1)
    _norm_phase(jj, h_ref, g_ref, hn_ref, nc)

    @pl.when(jj >= nc)
    def _():
        j = jj - nc
        w = w_ref[...].astype(BF16)
        is_rot = j < n_rot
        scale = jnp.where(jnp.logical_and(j >= n_q, is_rot), HEAD_DIM ** -0.5, 1.0).astype(F32)
        tp = hn_ref.shape[0] // parts
        for p in range(parts):
            rows = slice(p * tp, (p + 1) * tp)
            acc = jnp.dot(hn_ref[rows, :], w, preferred_element_type=F32)
            cos = jnp.where(is_rot, cos_ref[rows, :], 1.0) * scale
            sin = jnp.where(is_rot, sin_ref[rows, :], 0.0) * scale
            for hh in range(tn // HEAD_DIM):
                x1 = acc[:, hh * HEAD_DIM: hh * HEAD_DIM + HALF]
                x2 = acc[:, hh * HEAD_DIM + HALF: (hh + 1) * HEAD_DIM]
                out_ref[rows, hh * HEAD_DIM: hh * HEAD_DIM + HALF] = (x1 * cos - x2 * sin).astype(BF16)
                out_ref[rows, hh * HEAD_DIM + HALF: (hh + 1) * HEAD_DIM] = (x1 * sin + x2 * cos).astype(BF16)


def _in_proj(h, g, w, cos, sin, *, layer, tm, tn, nc, parts):
    m = h.shape[0]
    n = w.shape[2]
    assert tm % (BF16_ROWS * parts) == 0
    kern = functools.partial(_in_proj_kernel, nc=nc, tn=tn, n_q=RET_W // tn, n_rot=2 * RET_W // tn, parts=parts)
    h_spec, g_spec = _norm_specs(tm, nc)
    return pl.pallas_call(
        kern,
        grid=(m // tm, nc + n // tn),
        in_specs=[
            h_spec,
            g_spec,
            pl.BlockSpec((None, D_MODEL, tn), lambda i, jj: (layer, 0, _col(jj, nc))),
            pl.BlockSpec((tm, HALF), lambda i, jj: (i, 0)),
            pl.BlockSpec((tm, HALF), lambda i, jj: (i, 0)),
        ],
        out_specs=pl.BlockSpec((tm, tn), lambda i, jj: (i, _col(jj, nc))),
        out_shape=jax.ShapeDtypeStruct((m, n), BF16),
        scratch_shapes=[pltpu.VMEM((tm, D_MODEL), BF16)],
        compiler_params=_params(("arbitrary", "arbitrary")),
        name="in_proj",
    )(h, g, w, cos, sin)


def _mixer_kernel(q_ref, k_ref, v_ref, g_ref, p_ref, dmask_ref, qd_ref, kd_ref, cd_ref, band_ref, pw_ref, ps_ref,
                  out_ref, state_ref, pext_ref, *, chunk):
    c = pl.program_id(1)

    @pl.when(c == 0)
    def _():
        state_ref[...] = jnp.zeros_like(state_ref)
        pext_ref[0:POOL_HALO, :] = jnp.zeros((POOL_HALO, POOL_W), BF16)

    for hh in range(RET_HEADS):
        sl = slice(hh * HEAD_DIM, (hh + 1) * HEAD_DIM)
        q = q_ref[:, sl]
        k = k_ref[:, sl]
        v = v_ref[:, sl]
        scores = lax.dot_general(q, k, (((1,), (1,)), ((), ())), preferred_element_type=F32)
        scores = (scores * dmask_ref[hh]).astype(BF16)
        intra = jnp.dot(scores, v, preferred_element_type=F32)
        state = state_ref[hh]
        cross = jnp.dot(q, state.astype(BF16), preferred_element_type=F32) * qd_ref[hh]
        kdec = (k.astype(F32) * kd_ref[hh]).astype(BF16)
        upd = lax.dot_general(kdec, v, (((0,), (0,)), ((), ())), preferred_element_type=F32)
        state_ref[hh] = state * cd_ref[hh] + upd
        r = intra + cross
        r = r * lax.rsqrt(jnp.mean(r * r, axis=-1, keepdims=True) + EPS)
        gate = g_ref[:, sl].astype(F32)
        out_ref[:, sl] = (r * (gate * jax.nn.sigmoid(gate))).astype(BF16)

    pext_ref[POOL_HALO:POOL_HALO + chunk, :] = p_ref[...]
    t = c * chunk - META_PAD + lax.broadcasted_iota(jnp.int32, (chunk, 1), 0)
    valid = t >= 0
    for gi, w in enumerate(POOL_WINDOWS):
        cs = slice(gi * POOL_GROUP, (gi + 1) * POOL_GROUP)
        xg = p_ref[:, cs].astype(F32)
        win = jnp.dot(band_ref[gi], pext_ref[:, cs], preferred_element_type=F32)
        cnt = jnp.clip(t + 1, 1, w).astype(F32)
        mix = jnp.where(valid, win / cnt - xg, 0.0).astype(BF16)
        mg = jnp.dot(mix, pw_ref[gi], preferred_element_type=F32) * ps_ref[:, cs]
        out_ref[:, RET_W + gi * POOL_GROUP: RET_W + (gi + 1) * POOL_GROUP] = mg.astype(BF16)
    pext_ref[0:POOL_HALO, :] = pext_ref[chunk:chunk + POOL_HALO, :]


def _mixer(proj, dmask, qd, kd, cd, band, pool_w, pool_scale, *, batch, lp, chunk):
    nc = lp // chunk
    m = proj.shape[0]
    assert chunk >= POOL_HALO
    kern = functools.partial(_mixer_kernel, chunk=chunk)
    const3 = lambda b, c: (0, 0, 0)
    col_block = lambda n: pl.BlockSpec((chunk, RET_W), lambda b, c: (b * nc + c, n))
    return pl.pallas_call(
        kern,
        grid=(batch, nc),
        in_specs=[
            col_block(0), col_block(1), col_block(2), col_block(3), col_block(4),
            pl.BlockSpec((RET_HEADS, chunk, chunk), const3),
            pl.BlockSpec((RET_HEADS, chunk, 1), const3),
            pl.BlockSpec((RET_HEADS, chunk, 1), const3),
            pl.BlockSpec((RET_HEADS, 1, 1), const3),
            pl.BlockSpec((len(POOL_WINDOWS), chunk, POOL_HALO + chunk), const3),
            pl.BlockSpec((len(POOL_WINDOWS), POOL_GROUP, POOL_GROUP), const3),
            pl.BlockSpec((1, POOL_W), lambda b, c: (0, 0)),
        ],
        out_specs=pl.BlockSpec((chunk, RET_W + POOL_W), lambda b, c: (b * nc + c, 0)),
        out_shape=jax.ShapeDtypeStruct((m, RET_W + POOL_W), BF16),
        scratch_shapes=[
            pltpu.VMEM((RET_HEADS, HEAD_DIM, HEAD_DIM), F32),
            pltpu.VMEM((POOL_HALO + chunk, POOL_W), BF16),
        ],
        compiler_params=_params(("arbitrary", "arbitrary")),
        name="mixer",
    )(proj, proj, proj, proj, proj, dmask, qd, kd, cd, band, pool_w, pool_scale)


def _proj_res_kernel(lhs_ref, w_ref, res_ref, out_ref, *, parts):
    w = w_ref[...].astype(BF16)
    tp = lhs_ref.shape[0] // parts
    for p in range(parts):
        rows = slice(p * tp, (p + 1) * tp)
        out_ref[rows, :] = res_ref[rows, :] + jnp.dot(lhs_ref[rows, :], w, preferred_element_type=F32)


def _proj_res(lhs, w, res, *, layer, tm, tn, parts, name, k_blocks=1, k_block=0):
    m, k = lhs.shape
    n = w.shape[2]
    kb = k // k_blocks
    assert tm % (BF16_ROWS * parts) == 0 and k % k_blocks == 0 and kb % LANE == 0
    return pl.pallas_call(
        functools.partial(_proj_res_kernel, parts=parts),
        grid=(m // tm, n // tn),
        in_specs=[
            pl.BlockSpec((tm, kb), lambda i, j: (i, k_block), pipeline_mode=pl.Buffered(1)),
            pl.BlockSpec((None, kb, tn), lambda i, j: (layer, k_block, j)),
            pl.BlockSpec((tm, tn), lambda i, j: (i, j)),
        ],
        out_specs=pl.BlockSpec((tm, tn), lambda i, j: (i, j)),
        out_shape=jax.ShapeDtypeStruct((m, n), F32),
        compiler_params=_params(("arbitrary", "arbitrary")),
        name=name,
    )(lhs, w, res)


def _ffn_up_kernel(h_ref, g_ref, wa_ref, wb_ref, conv_ref, act_ref, hn_ref, aext_ref, carry_ref,
                   *, nc, tm, parts):
    i = pl.program_id(0)
    jj = pl.program_id(1)
    _norm_phase(jj, h_ref, g_ref, hn_ref, nc)

    @pl.when(jj >= nc)
    def _():
        j = jj - nc
        @pl.when(i == 0)
        def _():
            carry_ref[j] = jnp.zeros(carry_ref.shape[1:], F32)

        wa = wa_ref[...].astype(BF16)
        wb = wb_ref[...].astype(BF16)
        conv = conv_ref[j]
        cb = conv[CONV_W:CONV_W + 1, :]
        aext_ref[0:CONV_HALO, :] = carry_ref[j]
        tp = tm // parts
        for p in range(parts):
            rows = slice(p * tp, (p + 1) * tp)
            hn = hn_ref[rows, :]
            a = jnp.dot(hn, wa, preferred_element_type=F32)
            b = jnp.dot(hn, wb, preferred_element_type=F32)
            r0 = CONV_HALO + p * tp
            aext_ref[r0:r0 + tp, :] = a
            ac = cb + a * conv[CONV_W - 1:CONV_W, :]
            for tap in range(CONV_W - 1):
                back = CONV_W - 1 - tap
                ac = ac + aext_ref[r0 - back:r0 - back + tp, :] * conv[tap:tap + 1, :]
            act_ref[rows, :] = (ac * jax.nn.sigmoid(ac) * b).astype(BF16)
        carry_ref[j] = aext_ref[tm:tm + CONV_HALO, :]


def _ffn_up(h, g, w_up, conv_w, conv_b, *, layer, tm, tn, nc, parts):
    m = h.shape[0]
    n_j = D_FF // tn
    assert tm % (BF16_ROWS * parts) == 0
    conv = jnp.concatenate([conv_w, conv_b[None]], axis=0).reshape(CONV_W + 1, n_j, tn).transpose(1, 0, 2)
    kern = functools.partial(_ffn_up_kernel, nc=nc, tm=tm, parts=parts)
    h_spec, g_spec = _norm_specs(tm, nc)
    return pl.pallas_call(
        kern,
        grid=(m // tm, nc + n_j),
        in_specs=[
            h_spec,
            g_spec,
            pl.BlockSpec((None, D_MODEL, tn), lambda i, jj: (layer, 0, _col(jj, nc))),
            pl.BlockSpec((None, D_MODEL, tn), lambda i, jj: (layer, 0, n_j + _col(jj, nc))),
            pl.BlockSpec((n_j, CONV_W + 1, tn), lambda i, jj: (0, 0, 0)),
        ],
        out_specs=pl.BlockSpec((tm, tn), lambda i, jj: (i, _col(jj, nc))),
        out_shape=jax.ShapeDtypeStruct((m, D_FF), BF16),
        scratch_shapes=[
            pltpu.VMEM((tm, D_MODEL), BF16),
            pltpu.VMEM((CONV_HALO + tm, tn), F32),
            pltpu.VMEM((n_j, CONV_HALO, tn), F32),
        ],
        compiler_params=_params(("arbitrary", "arbitrary")),
        name="ffn_up",
    )(h, g, w_up, w_up, conv)


def _final_norm_kernel(h_ref, g_ref, out_ref):
    out_ref[0] = _rms_rows(h_ref[...], g_ref[...])


def _final_norm(h, g, *, batch, seq, lp, tr):
    skip = META_PAD + N_META
    assert seq % tr == 0
    return pl.pallas_call(
        _final_norm_kernel,
        grid=(batch, seq // tr),
        in_specs=[
            pl.BlockSpec((pl.Element(tr), pl.Element(D_MODEL)), lambda b, c: (pl.multiple_of(b * lp + skip + c * tr, ROW_ALIGN), 0)),
            pl.BlockSpec((1, D_MODEL), lambda b, c: (0, 0)),
        ],
        out_specs=pl.BlockSpec((1, tr, D_MODEL), lambda b, c: (b, c, 0)),
        out_shape=jax.ShapeDtypeStruct((batch, seq, D_MODEL), F32),
        compiler_params=_params(("arbitrary", "arbitrary")),
        name="final_norm",
    )(h, g)


def _decay_tables(chunk):
    log_gamma = jnp.log1p(-jnp.exp2(-5.0 - jnp.arange(RET_HEADS, dtype=F32)))
    idx = jnp.arange(chunk, dtype=F32)
    rel = idx[:, None] - idx[None, :]
    dmask = jnp.where(rel[None] >= 0, jnp.exp(log_gamma[:, None, None] * jnp.maximum(rel, 0.0)[None]), 0.0)
    qd = jnp.exp(log_gamma[:, None] * (idx + 1.0)[None, :])[:, :, None]
    kd = jnp.exp(log_gamma[:, None] * (chunk - 1.0 - idx)[None, :])[:, :, None]
    cd = jnp.exp(log_gamma * chunk)[:, None, None]
    return dmask, qd, kd, cd


def _pool_band(chunk):
    r = jnp.arange(chunk)[:, None] + POOL_HALO
    c = jnp.arange(POOL_HALO + chunk)[None, :]
    lag = r - c
    return jnp.stack([jnp.logical_and(lag >= 0, lag < w) for w in POOL_WINDOWS]).astype(BF16)


def kernel(x, meta_tokens, norm1_g, w_in, pool_w, pool_scale, w_out, norm2_g, w_up, conv_w, conv_b, w_down, final_g):
    batch, seq, d = x.shape
    depth = w_in.shape[0]
    assert d == D_MODEL and meta_tokens.shape == (N_META, D_MODEL)
    assert w_up.shape[2] == 2 * D_FF and w_down.shape[1] == D_FF
    lp = META_PAD + N_META + seq
    m = batch * lp
    tm_norm = m // 4
    tm_out = m // 3
    tm_down = m // 4
    chunk = 384
    assert lp % chunk == 0
    for t in (tm_norm, tm_out, tm_down):
        assert m % t == 0

    meta = jnp.broadcast_to(meta_tokens[None].astype(x.dtype), (batch, N_META, d))
    h = jnp.concatenate([jnp.zeros((batch, META_PAD, d), x.dtype), meta, x], axis=1).reshape(m, d)

    pos = (jnp.arange(lp) - META_PAD).astype(F32)
    inv = jnp.power(ROPE_BASE, -jnp.arange(HALF, dtype=F32) / HALF)
    ang = pos[:, None] * inv[None, :]
    cos = jnp.tile(jnp.cos(ang), (batch, 1))
    sin = jnp.tile(jnp.sin(ang), (batch, 1))
    dmask, qd, kd, cd = _decay_tables(chunk)
    band = _pool_band(chunk)

    for l in range(depth):
        proj = _in_proj(h, norm1_g[l][None], w_in, cos, sin, layer=l, tm=tm_norm, tn=512, nc=6, parts=3)
        mixed = _mixer(proj, dmask, qd, kd, cd, band, pool_w[l].astype(BF16), pool_scale[l][None],
                       batch=batch, lp=lp, chunk=chunk)
        h = _proj_res(mixed, w_out, h, layer=l, tm=tm_out, tn=256, parts=4, name="out_proj")
        act = _ffn_up(h, norm2_g[l][None], w_up, conv_w[l], conv_b[l], layer=l, tm=tm_norm, tn=256, nc=6, parts=4)
        for kb in range(2):
            h = _proj_res(act, w_down, h, layer=l, tm=tm_down, tn=256, parts=4, name="ffn_down",
                          k_blocks=2, k_block=kb)

    return _final_norm(h, final_g[None], batch=batch, seq=seq, lp=lp, tr=512)
```

```python
import functools

import jax
import jax.numpy as jnp
from jax import lax
from jax.experimental import pallas as pl
from jax.experimental.pallas import tpu as pltpu

D_MODEL = 4096
N_META = 16
RET_W = 2048
POOL_W = 2048
RET_HEADS = 8
HEAD_DIM = 256
HALF = HEAD_DIM // 2
POOL_WINDOWS = (2, 4, 8, 16)
POOL_GROUP = 512
D_FF = 11008
CONV_W = 3
ROPE_BASE = 10000.0
EPS = 1e-6

LANE = 128
BF16_ROWS = 16
ROW_ALIGN = 128
META_PAD = (-N_META) % ROW_ALIGN
POOL_HALO = 128
CONV_HALO = 8
VMEM_LIMIT = 58 * 1024 * 1024

BF16 = jnp.bfloat16
F32 = jnp.float32


def _params(sem):
    return pltpu.CompilerParams(dimension_semantics=sem, vmem_limit_bytes=VMEM_LIMIT)


def _rms_rows(x, g):
    return x * lax.rsqrt(jnp.mean(x * x, axis=-1, keepdims=True) + EPS) * g


def _norm_phase(jj, h_ref, g_ref, hn_ref, nc):
    rc = hn_ref.shape[0] // nc

    @pl.when(jj < nc)
    def _():
        r0 = pl.multiple_of(jj * rc, BF16_ROWS)
        hn_ref[pl.ds(r0, rc), :] = _rms_rows(h_ref[...], g_ref[...]).astype(BF16)


def _norm_specs(tm, nc):
    assert tm % (BF16_ROWS * nc) == 0
    h_spec = pl.BlockSpec((tm // nc, D_MODEL), lambda i, jj: (i * nc + jnp.minimum(jj, nc - 1), 0))
    g_spec = pl.BlockSpec((1, D_MODEL), lambda i, jj: (0, 0))
    return h_spec, g_spec


def _col(jj, nc):
    return jnp.maximum(jj - nc, 0)


def _in_proj_kernel(h_ref, g_ref, w_ref, cos_ref, sin_ref, out_ref, hn_ref, *, nc, tn, n_q, n_rot, parts):
    jj = pl.program_id(1)
    _norm_phase(jj, h_ref, g_ref, hn_ref, nc)

    @pl.when(jj >= nc)
    def _():
        j = jj - nc
        w = w_ref[...].astype(BF16)
        is_rot = j < n_rot
        scale = jnp.where(jnp.logical_and(j >= n_q, is_rot), HEAD_DIM ** -0.5, 1.0).astype(F32)
        tp = hn_ref.shape[0] // parts
        for p in range(parts):
            rows = slice(p * tp, (p + 1) * tp)
            acc = jnp.dot(hn_ref[rows, :], w, preferred_element_type=F32)
            cos = jnp.where(is_rot, cos_ref[rows, :], 1.0) * scale
            sin = jnp.where(is_rot, sin_ref[rows, :], 0.0) * scale
            for hh in range(tn // HEAD_DIM):
                x1 = acc[:, hh * HEAD_DIM: hh * HEAD_DIM + HALF]
                x2 = acc[:, hh * HEAD_DIM + HALF: (hh + 1) * HEAD_DIM]
                out_ref[rows, hh * HEAD_DIM: hh * HEAD_DIM + HALF] = (x1 * cos - x2 * sin).astype(BF16)
                out_ref[rows, hh * HEAD_DIM + HALF: (hh + 1) * HEAD_DIM] = (x1 * sin + x2 * cos).astype(BF16)


def _in_proj(h, g, w, cos, sin, *, layer, tm, tn, nc, parts):
    m = h.shape[0]
    n = w.shape[2]
    assert tm % (BF16_ROWS * parts) == 0
    kern = functools.partial(_in_proj_kernel, nc=nc, tn=tn, n_q=RET_W // tn, n_rot=2 * RET_W // tn, parts=parts)
    h_spec, g_spec = _norm_specs(tm, nc)
    return pl.pallas_call(
        kern,
        grid=(m // tm, nc + n // tn),
        in_specs=[
            h_spec,
            g_spec,
            pl.BlockSpec((None, D_MODEL, tn), lambda i, jj: (layer, 0, _col(jj, nc))),
            pl.BlockSpec((tm, HALF), lambda i, jj: (i, 0)),
            pl.BlockSpec((tm, HALF), lambda i, jj: (i, 0)),
        ],
        out_specs=pl.BlockSpec((tm, tn), lambda i, jj: (i, _col(jj, nc))),
        out_shape=jax.ShapeDtypeStruct((m, n), BF16),
        scratch_shapes=[pltpu.VMEM((tm, D_MODEL), BF16)],
        compiler_params=_params(("arbitrary", "arbitrary")),
        name="in_proj",
    )(h, g, w, cos, sin)


def _mixer_kernel(q_ref, k_ref, v_ref, g_ref, p_ref, dmask_ref, qd_ref, kd_ref, cd_ref, band_ref, pw_ref, ps_ref,
                  out_ref, state_ref, pext_ref, *, chunk):
    c = pl.program_id(1)

    @pl.when(c == 0)
    def _():
        state_ref[...] = jnp.zeros_like(state_ref)
        pext_ref[0:POOL_HALO, :] = jnp.zeros((POOL_HALO, POOL_W), BF16)

    for hh in range(RET_HEADS):
        sl = slice(hh * HEAD_DIM, (hh + 1) * HEAD_DIM)
        q = q_ref[:, sl]
        k = k_ref[:, sl]
        v = v_ref[:, sl]
        scores = lax.dot_general(q, k, (((1,), (1,)), ((), ())), preferred_element_type=F32)
        scores = (scores * dmask_ref[hh]).astype(BF16)
        intra = jnp.dot(scores, v, preferred_element_type=F32)
        state = state_ref[hh]
        cross = jnp.dot(q, state.astype(BF16), preferred_element_type=F32) * qd_ref[hh]
        kdec = (k.astype(F32) * kd_ref[hh]).astype(BF16)
        upd = lax.dot_general(kdec, v, (((0,), (0,)), ((), ())), preferred_element_type=F32)
        state_ref[hh] = state * cd_ref[hh] + upd
        r = intra + cross
        r = r * lax.rsqrt(jnp.mean(r * r, axis=-1, keepdims=True) + EPS)
        gate = g_ref[:, sl].astype(F32)
        out_ref[:, sl] = (r * (gate * jax.nn.sigmoid(gate))).astype(BF16)

    pext_ref[POOL_HALO:POOL_HALO + chunk, :] = p_ref[...]
    t = c * chunk - META_PAD + lax.broadcasted_iota(jnp.int32, (chunk, 1), 0)
    valid = t >= 0
    for gi, w in enumerate(POOL_WINDOWS):
        cs = slice(gi * POOL_GROUP, (gi + 1) * POOL_GROUP)
        xg = p_ref[:, cs].astype(F32)
        win = jnp.dot(band_ref[gi], pext_ref[:, cs], preferred_element_type=F32)
        cnt = jnp.clip(t + 1, 1, w).astype(F32)
        mix = jnp.where(valid, win / cnt - xg, 0.0).astype(BF16)
        mg = jnp.dot(mix, pw_ref[gi], preferred_element_type=F32) * ps_ref[:, cs]
        out_ref[:, RET_W + gi * POOL_GROUP: RET_W + (gi + 1) * POOL_GROUP] = mg.astype(BF16)
    pext_ref[0:POOL_HALO, :] = pext_ref[chunk:chunk + POOL_HALO, :]


def _mixer(proj, dmask, qd, kd, cd, band, pool_w, pool_scale, *, batch, lp, chunk):
    nc = lp // chunk
    m = proj.shape[0]
    assert chunk >= POOL_HALO
    kern = functools.partial(_mixer_kernel, chunk=chunk)
    const3 = lambda b, c: (0, 0, 0)
    col_block = lambda n: pl.BlockSpec((chunk, RET_W), lambda b, c: (b * nc + c, n))
    return pl.pallas_call(
        kern,
        grid=(batch, nc),
        in_specs=[
            col_block(0), col_block(1), col_block(2), col_block(3), col_block(4),
            pl.BlockSpec((RET_HEADS, chunk, chunk), const3),
            pl.BlockSpec((RET_HEADS, chunk, 1), const3),
            pl.BlockSpec((RET_HEADS, chunk, 1), const3),
            pl.BlockSpec((RET_HEADS, 1, 1), const3),
            pl.BlockSpec((len(POOL_WINDOWS), chunk, POOL_HALO + chunk), const3),
            pl.BlockSpec((len(POOL_WINDOWS), POOL_GROUP, POOL_GROUP), const3),
            pl.BlockSpec((1, POOL_W), lambda b, c: (0, 0)),
        ],
        out_specs=pl.BlockSpec((chunk, RET_W + POOL_W), lambda b, c: (b * nc + c, 0)),
        out_shape=jax.ShapeDtypeStruct((m, RET_W + POOL_W), BF16),
        scratch_shapes=[
            pltpu.VMEM((RET_HEADS, HEAD_DIM, HEAD_DIM), F32),
            pltpu.VMEM((POOL_HALO + chunk, POOL_W), BF16),
        ],
        compiler_params=_params(("arbitrary", "arbitrary")),
        name="mixer",
    )(proj, proj, proj, proj, proj, dmask, qd, kd, cd, band, pool_w, pool_scale)


def _proj_res_kernel(lhs_ref, w_ref, res_ref, out_ref, *, parts):
    w = w_ref[...].astype(BF16)
    tp = lhs_ref.shape[0] // parts
    for p in range(parts):
        rows = slice(p * tp, (p + 1) * tp)
        out_ref[rows, :] = res_ref[rows, :] + jnp.dot(lhs_ref[rows, :], w, preferred_element_type=F32)


def _proj_res(lhs, w, res, *, layer, tm, tn, parts, name, k_blocks=1, k_block=0, lhs_buffers=1):
    m, k = lhs.shape
    n = w.shape[2]
    kb = k // k_blocks
    assert tm % (BF16_ROWS * parts) == 0 and k % k_blocks == 0 and kb % LANE == 0
    return pl.pallas_call(
        functools.partial(_proj_res_kernel, parts=parts),
        grid=(m // tm, n // tn),
        in_specs=[
            pl.BlockSpec((tm, kb), lambda i, j: (i, k_block), pipeline_mode=pl.Buffered(lhs_buffers)),
            pl.BlockSpec((None, kb, tn), lambda i, j: (layer, k_block, j)),
            pl.BlockSpec((tm, tn), lambda i, j: (i, j)),
        ],
        out_specs=pl.BlockSpec((tm, tn), lambda i, j: (i, j)),
        out_shape=jax.ShapeDtypeStruct((m, n), F32),
        compiler_params=_params(("arbitrary", "arbitrary")),
        name=name,
    )(lhs, w, res)


def _ffn_up_kernel(h_ref, g_ref, wa_ref, wb_ref, conv_ref, act_ref, hn_ref, aext_ref, carry_ref,
                   *, nc, tm, parts):
    i = pl.program_id(0)
    jj = pl.program_id(1)
    _norm_phase(jj, h_ref, g_ref, hn_ref, nc)

    @pl.when(jj >= nc)
    def _():
        j = jj - nc
        @pl.when(i == 0)
        def _():
            carry_ref[j] = jnp.zeros(carry_ref.shape[1:], F32)

        wa = wa_ref[...].astype(BF16)
        wb = wb_ref[...].astype(BF16)
        conv = conv_ref[j]
        cb = conv[CONV_W:CONV_W + 1, :]
        aext_ref[0:CONV_HALO, :] = carry_ref[j]
        start = 0
        for tp in parts:
            rows = slice(start, start + tp)
            hn = hn_ref[rows, :]
            a = jnp.dot(hn, wa, preferred_element_type=F32)
            b = jnp.dot(hn, wb, preferred_element_type=F32)
            r0 = CONV_HALO + start
            start += tp
            aext_ref[r0:r0 + tp, :] = a
            ac = cb + a * conv[CONV_W - 1:CONV_W, :]
            for tap in range(CONV_W - 1):
                back = CONV_W - 1 - tap
                ac = ac + aext_ref[r0 - back:r0 - back + tp, :] * conv[tap:tap + 1, :]
            act_ref[rows, :] = (ac * jax.nn.sigmoid(ac) * b).astype(BF16)
        carry_ref[j] = aext_ref[tm:tm + CONV_HALO, :]


def _ffn_up(h, g, w_up, conv_w, conv_b, *, layer, tm, tn, nc, parts):
    m = h.shape[0]
    n_j = D_FF // tn
    assert sum(parts) == tm and all(tp % BF16_ROWS == 0 for tp in parts)
    conv = jnp.concatenate([conv_w, conv_b[None]], axis=0).reshape(CONV_W + 1, n_j, tn).transpose(1, 0, 2)
    kern = functools.partial(_ffn_up_kernel, nc=nc, tm=tm, parts=parts)
    h_spec, g_spec = _norm_specs(tm, nc)
    return pl.pallas_call(
        kern,
        grid=(m // tm, nc + n_j),
        in_specs=[
            h_spec,
            g_spec,
            pl.BlockSpec((None, D_MODEL, tn), lambda i, jj: (layer, 0, _col(jj, nc))),
            pl.BlockSpec((None, D_MODEL, tn), lambda i, jj: (layer, 0, n_j + _col(jj, nc))),
            pl.BlockSpec((n_j, CONV_W + 1, tn), lambda i, jj: (0, 0, 0)),
        ],
        out_specs=pl.BlockSpec((tm, tn), lambda i, jj: (i, _col(jj, nc))),
        out_shape=jax.ShapeDtypeStruct((m, D_FF), BF16),
        scratch_shapes=[
            pltpu.VMEM((tm, D_MODEL), BF16),
            pltpu.VMEM((CONV_HALO + tm, tn), F32),
            pltpu.VMEM((n_j, CONV_HALO, tn), F32),
        ],
        compiler_params=_params(("arbitrary", "arbitrary")),
        name="ffn_up",
    )(h, g, w_up, w_up, conv)


def _final_norm_kernel(h_ref, g_ref, out_ref):
    out_ref[0] = _rms_rows(h_ref[...], g_ref[...])


def _final_norm(h, g, *, batch, seq, lp, tr):
    skip = META_PAD + N_META
    assert seq % tr == 0
    return pl.pallas_call(
        _final_norm_kernel,
        grid=(batch, seq // tr),
        in_specs=[
            pl.BlockSpec((pl.Element(tr), pl.Element(D_MODEL)), lambda b, c: (pl.multiple_of(b * lp + skip + c * tr, ROW_ALIGN), 0)),
            pl.BlockSpec((1, D_MODEL), lambda b, c: (0, 0)),
        ],
        out_specs=pl.BlockSpec((1, tr, D_MODEL), lambda b, c: (b, c, 0)),
        out_shape=jax.ShapeDtypeStruct((batch, seq, D_MODEL), F32),
        compiler_params=_params(("arbitrary", "arbitrary")),
        name="final_norm",
    )(h, g)


def _decay_tables(chunk):
    log_gamma = jnp.log1p(-jnp.exp2(-5.0 - jnp.arange(RET_HEADS, dtype=F32)))
    idx = jnp.arange(chunk, dtype=F32)
    rel = idx[:, None] - idx[None, :]
    dmask = jnp.where(rel[None] >= 0, jnp.exp(log_gamma[:, None, None] * jnp.maximum(rel, 0.0)[None]), 0.0)
    qd = jnp.exp(log_gamma[:, None] * (idx + 1.0)[None, :])[:, :, None]
    kd = jnp.exp(log_gamma[:, None] * (chunk - 1.0 - idx)[None, :])[:, :, None]
    cd = jnp.exp(log_gamma * chunk)[:, None, None]
    return dmask, qd, kd, cd


def _pool_band(chunk):
    r = jnp.arange(chunk)[:, None] + POOL_HALO
    c = jnp.arange(POOL_HALO + chunk)[None, :]
    lag = r - c
    return jnp.stack([jnp.logical_and(lag >= 0, lag < w) for w in POOL_WINDOWS]).astype(BF16)


def kernel(x, meta_tokens, norm1_g, w_in, pool_w, pool_scale, w_out, norm2_g, w_up, conv_w, conv_b, w_down, final_g):
    batch, seq, d = x.shape
    depth = w_in.shape[0]
    assert d == D_MODEL and meta_tokens.shape == (N_META, D_MODEL)
    assert w_up.shape[2] == 2 * D_FF and w_down.shape[1] == D_FF
    lp = META_PAD + N_META + seq
    m = batch * lp
    tm_norm = m // 4
    tm_out = m // 3
    tm_down = m // 4
    chunk = 384
    assert lp % chunk == 0
    for t in (tm_norm, tm_out, tm_down):
        assert m % t == 0

    meta = jnp.broadcast_to(meta_tokens[None].astype(x.dtype), (batch, N_META, d))
    h = jnp.concatenate([jnp.zeros((batch, META_PAD, d), x.dtype), meta, x], axis=1).reshape(m, d)

    pos = (jnp.arange(lp) - META_PAD).astype(F32)
    inv = jnp.power(ROPE_BASE, -jnp.arange(HALF, dtype=F32) / HALF)
    ang = pos[:, None] * inv[None, :]
    cos = jnp.tile(jnp.cos(ang), (batch, 1))
    sin = jnp.tile(jnp.sin(ang), (batch, 1))
    dmask, qd, kd, cd = _decay_tables(chunk)
    band = _pool_band(chunk)

    for l in range(depth):
        proj = _in_proj(h, norm1_g[l][None], w_in, cos, sin, layer=l, tm=tm_norm, tn=512, nc=6, parts=3)
        mixed = _mixer(proj, dmask, qd, kd, cd, band, pool_w[l].astype(BF16), pool_scale[l][None],
                       batch=batch, lp=lp, chunk=chunk)
        ab = l % 2
        h = _proj_res(mixed, w_out, h, layer=l, tm=(tm_out, m // 4)[ab], tn=256, parts=4, name="out_proj",
                      lhs_buffers=(1, 2)[ab])
        act = _ffn_up(h, norm2_g[l][None], w_up, conv_w[l], conv_b[l], layer=l, tm=tm_norm, tn=256, nc=6,
                      parts=((528, 528, 528, 528), (704, 704, 352, 352))[ab])
        for kb in range(2):
            h = _proj_res(act, w_down, h, layer=l, tm=(tm_down, m // 6)[ab], tn=256, parts=4, name="ffn_down",
                          k_blocks=2, k_block=kb, lhs_buffers=(1, 2)[ab])

    return _final_norm(h, final_g[None], batch=batch, seq=seq, lp=lp, tr=512)
```

```python
import functools

import jax
import jax.numpy as jnp
from jax import lax
from jax.experimental import pallas as pl
from jax.experimental.pallas import tpu as pltpu

D_MODEL = 4096
N_META = 16
RET_W = 2048
POOL_W = 2048
RET_HEADS = 8
HEAD_DIM = 256
HALF = HEAD_DIM // 2
POOL_WINDOWS = (2, 4, 8, 16)
POOL_GROUP = 512
D_FF = 11008
CONV_W = 3
ROPE_BASE = 10000.0
EPS = 1e-6

LANE = 128
BF16_ROWS = 16
ROW_ALIGN = 128
META_PAD = (-N_META) % ROW_ALIGN
POOL_HALO = 128
CONV_HALO = 8
VMEM_LIMIT = 58 * 1024 * 1024

BF16 = jnp.bfloat16
F32 = jnp.float32


def _params(sem):
    return pltpu.CompilerParams(dimension_semantics=sem, vmem_limit_bytes=VMEM_LIMIT)


def _rms_rows(x, g):
    return x * lax.rsqrt(jnp.mean(x * x, axis=-1, keepdims=True) + EPS) * g


def _norm_phase(jj, h_ref, g_ref, hn_ref, nc):
    rc = hn_ref.shape[0] // nc

    @pl.when(jj < nc)
    def _():
        r0 = pl.multiple_of(jj * rc, BF16_ROWS)
        hn_ref[pl.ds(r0, rc), :] = _rms_rows(h_ref[...], g_ref[...]).astype(BF16)


def _norm_specs(tm, nc):
    assert tm % (BF16_ROWS * nc) == 0
    h_spec = pl.BlockSpec((tm // nc, D_MODEL), lambda i, jj: (i * nc + jnp.minimum(jj, nc - 1), 0))
    g_spec = pl.BlockSpec((1, D_MODEL), lambda i, jj: (0, 0))
    return h_spec, g_spec


def _col(jj, nc):
    return jnp.maximum(jj - nc, 0)


def _in_proj_kernel(h_ref, g_ref, w_ref, cos_ref, sin_ref, out_ref, hn_ref, *, nc, tn, n_q, n_rot, parts):
    jj = pl.program_id(1)
    _norm_phase(jj, h_ref, g_ref, hn_ref, nc)

    @pl.when(jj >= nc)
    def _():
        j = jj - nc
        w = w_ref[...].astype(BF16)
        is_rot = j < n_rot
        scale = jnp.where(jnp.logical_and(j >= n_q, is_rot), HEAD_DIM ** -0.5, 1.0).astype(F32)
        tp = hn_ref.shape[0] // parts
        for p in range(parts):
            rows = slice(p * tp, (p + 1) * tp)
            acc = jnp.dot(hn_ref[rows, :], w, preferred_element_type=F32)
            cos = jnp.where(is_rot, cos_ref[rows, :], 1.0) * scale
            sin = jnp.where(is_rot, sin_ref[rows, :], 0.0) * scale
            for hh in range(tn // HEAD_DIM):
                x1 = acc[:, hh * HEAD_DIM: hh * HEAD_DIM + HALF]
                x2 = acc[:, hh * HEAD_DIM + HALF: (hh + 1) * HEAD_DIM]
                out_ref[rows, hh * HEAD_DIM: hh * HEAD_DIM + HALF] = (x1 * cos - x2 * sin).astype(BF16)
                out_ref[rows, hh * HEAD_DIM + HALF: (hh + 1) * HEAD_DIM] = (x1 * sin + x2 * cos).astype(BF16)


def _in_proj(h, g, w, cos, sin, *, layer, tm, tn, nc, parts):
    m = h.shape[0]
    n = w.shape[2]
    assert tm % (BF16_ROWS * parts) == 0
    kern = functools.partial(_in_proj_kernel, nc=nc, tn=tn, n_q=RET_W // tn, n_rot=2 * RET_W // tn, parts=parts)
    h_spec, g_spec = _norm_specs(tm, nc)
    return pl.pallas_call(
        kern,
        grid=(m // tm, nc + n // tn),
        in_specs=[
            h_spec,
            g_spec,
            pl.BlockSpec((None, D_MODEL, tn), lambda i, jj: (layer, 0, _col(jj, nc))),
            pl.BlockSpec((tm, HALF), lambda i, jj: (i, 0)),
            pl.BlockSpec((tm, HALF), lambda i, jj: (i, 0)),
        ],
        out_specs=pl.BlockSpec((tm, tn), lambda i, jj: (i, _col(jj, nc))),
        out_shape=jax.ShapeDtypeStruct((m, n), BF16),
        scratch_shapes=[pltpu.VMEM((tm, D_MODEL), BF16)],
        compiler_params=_params(("arbitrary", "arbitrary")),
        name="in_proj",
    )(h, g, w, cos, sin)


def _mixer_kernel(q_ref, k_ref, v_ref, g_ref, p_ref, dmask_ref, qd_ref, kd_ref, cd_ref, band_ref, pw_ref, ps_ref,
                  out_ref, state_ref, pext_ref, *, chunk):
    c = pl.program_id(1)

    @pl.when(c == 0)
    def _():
        state_ref[...] = jnp.zeros_like(state_ref)
        pext_ref[0:POOL_HALO, :] = jnp.zeros((POOL_HALO, POOL_W), BF16)

    for hh in range(RET_HEADS):
        sl = slice(hh * HEAD_DIM, (hh + 1) * HEAD_DIM)
        q = q_ref[:, sl]
        k = k_ref[:, sl]
        v = v_ref[:, sl]
        scores = lax.dot_general(q, k, (((1,), (1,)), ((), ())), preferred_element_type=F32)
        scores = (scores * dmask_ref[hh]).astype(BF16)
        intra = jnp.dot(scores, v, preferred_element_type=F32)
        state = state_ref[hh]
        cross = jnp.dot(q, state.astype(BF16), preferred_element_type=F32) * qd_ref[hh]
        kdec = (k.astype(F32) * kd_ref[hh]).astype(BF16)
        upd = lax.dot_general(kdec, v, (((0,), (0,)), ((), ())), preferred_element_type=F32)
        state_ref[hh] = state * cd_ref[hh] + upd
        r = intra + cross
        r = r * lax.rsqrt(jnp.mean(r * r, axis=-1, keepdims=True) + EPS)
        gate = g_ref[:, sl].astype(F32)
        out_ref[:, sl] = (r * (gate * jax.nn.sigmoid(gate))).astype(BF16)

    pext_ref[POOL_HALO:POOL_HALO + chunk, :] = p_ref[...]
    t = c * chunk - META_PAD + lax.broadcasted_iota(jnp.int32, (chunk, 1), 0)
    valid = t >= 0
    for gi, w in enumerate(POOL_WINDOWS):
        cs = slice(gi * POOL_GROUP, (gi + 1) * POOL_GROUP)
        xg = p_ref[:, cs].astype(F32)
        win = jnp.dot(band_ref[gi], pext_ref[:, cs], preferred_element_type=F32)
        cnt = jnp.clip(t + 1, 1, w).astype(F32)
        mix = jnp.where(valid, win / cnt - xg, 0.0).astype(BF16)
        mg = jnp.dot(mix, pw_ref[gi], preferred_element_type=F32) * ps_ref[:, cs]
        out_ref[:, RET_W + gi * POOL_GROUP: RET_W + (gi + 1) * POOL_GROUP] = mg.astype(BF16)
    pext_ref[0:POOL_HALO, :] = pext_ref[chunk:chunk + POOL_HALO, :]


def _mixer(proj, dmask, qd, kd, cd, band, pool_w, pool_scale, *, batch, lp, chunk):
    nc = lp // chunk
    m = proj.shape[0]
    assert chunk >= POOL_HALO
    kern = functools.partial(_mixer_kernel, chunk=chunk)
    const3 = lambda b, c: (0, 0, 0)
    col_block = lambda n: pl.BlockSpec((chunk, RET_W), lambda b, c: (b * nc + c, n))
    return pl.pallas_call(
        kern,
        grid=(batch, nc),
        in_specs=[
            col_block(0), col_block(1), col_block(2), col_block(3), col_block(4),
            pl.BlockSpec((RET_HEADS, chunk, chunk), const3),
            pl.BlockSpec((RET_HEADS, chunk, 1), const3),
            pl.BlockSpec((RET_HEADS, chunk, 1), const3),
            pl.BlockSpec((RET_HEADS, 1, 1), const3),
            pl.BlockSpec((len(POOL_WINDOWS), chunk, POOL_HALO + chunk), const3),
            pl.BlockSpec((len(POOL_WINDOWS), POOL_GROUP, POOL_GROUP), const3),
            pl.BlockSpec((1, POOL_W), lambda b, c: (0, 0)),
        ],
        out_specs=pl.BlockSpec((chunk, RET_W + POOL_W), lambda b, c: (b * nc + c, 0)),
        out_shape=jax.ShapeDtypeStruct((m, RET_W + POOL_W), BF16),
        scratch_shapes=[
            pltpu.VMEM((RET_HEADS, HEAD_DIM, HEAD_DIM), F32),
            pltpu.VMEM((POOL_HALO + chunk, POOL_W), BF16),
        ],
        compiler_params=_params(("arbitrary", "arbitrary")),
        name="mixer",
    )(proj, proj, proj, proj, proj, dmask, qd, kd, cd, band, pool_w, pool_scale)


def _proj_res_kernel(lhs_ref, w_ref, res_ref, out_ref, *, parts):
    w = w_ref[...].astype(BF16)
    tp = lhs_ref.shape[0] // parts
    for p in range(parts):
        rows = slice(p * tp, (p + 1) * tp)
        out_ref[rows, :] = res_ref[rows, :] + jnp.dot(lhs_ref[rows, :], w, preferred_element_type=F32)


def _proj_res(lhs, w, res, *, layer, tm, tn, parts, name, k_blocks=1, k_block=0):
    m, k = lhs.shape
    n = w.shape[2]
    kb = k // k_blocks
    assert tm % (BF16_ROWS * parts) == 0 and k % k_blocks == 0 and kb % LANE == 0
    return pl.pallas_call(
        functools.partial(_proj_res_kernel, parts=parts),
        grid=(m // tm, n // tn),
        in_specs=[
            pl.BlockSpec((tm, kb), lambda i, j: (i, k_block)),
            pl.BlockSpec((None, kb, tn), lambda i, j: (layer, k_block, j)),
            pl.BlockSpec((tm, tn), lambda i, j: (i, j)),
        ],
        out_specs=pl.BlockSpec((tm, tn), lambda i, j: (i, j)),
        out_shape=jax.ShapeDtypeStruct((m, n), F32),
        compiler_params=_params(("arbitrary", "arbitrary")),
        name=name,
    )(lhs, w, res)


def _ffn_up_kernel(h_ref, g_ref, wa_ref, wb_ref, conv_ref, act_ref, hn_ref, aext_ref, carry_ref,
                   *, nc, tm, parts):
    i = pl.program_id(0)
    jj = pl.program_id(1)
    _norm_phase(jj, h_ref, g_ref, hn_ref, nc)

    @pl.when(jj >= nc)
    def _():
        j = jj - nc
        @pl.when(i == 0)
        def _():
            carry_ref[j] = jnp.zeros(carry_ref.shape[1:], F32)

        wa = wa_ref[...].astype(BF16)
        wb = wb_ref[...].astype(BF16)
        conv = conv_ref[j]
        cb = conv[CONV_W:CONV_W + 1, :]
        aext_ref[0:CONV_HALO, :] = carry_ref[j]
        start = 0
        for tp in parts:
            rows = slice(start, start + tp)
            hn = hn_ref[rows, :]
            a = jnp.dot(hn, wa, preferred_element_type=F32)
            b = jnp.dot(hn, wb, preferred_element_type=F32)
            r0 = CONV_HALO + start
            start += tp
            aext_ref[r0:r0 + tp, :] = a
            ac = cb + a * conv[CONV_W - 1:CONV_W, :]
            for tap in range(CONV_W - 1):
                back = CONV_W - 1 - tap
                ac = ac + aext_ref[r0 - back:r0 - back + tp, :] * conv[tap:tap + 1, :]
            act_ref[rows, :] = (ac * jax.nn.sigmoid(ac) * b).astype(BF16)
        carry_ref[j] = aext_ref[tm:tm + CONV_HALO, :]


def _ffn_up(h, g, w_up, conv_w, conv_b, *, layer, tm, tn, nc, parts):
    m = h.shape[0]
    n_j = D_FF // tn
    assert sum(parts) == tm and all(tp % BF16_ROWS == 0 for tp in parts)
    conv = jnp.concatenate([conv_w, conv_b[None]], axis=0).reshape(CONV_W + 1, n_j, tn).transpose(1, 0, 2)
    kern = functools.partial(_ffn_up_kernel, nc=nc, tm=tm, parts=parts)
    h_spec, g_spec = _norm_specs(tm, nc)
    return pl.pallas_call(
        kern,
        grid=(m // tm, nc + n_j),
        in_specs=[
            h_spec,
            g_spec,
            pl.BlockSpec((None, D_MODEL, tn), lambda i, jj: (layer, 0, _col(jj, nc))),
            pl.BlockSpec((None, D_MODEL, tn), lambda i, jj: (layer, 0, n_j + _col(jj, nc))),
            pl.BlockSpec((n_j, CONV_W + 1, tn), lambda i, jj: (0, 0, 0)),
        ],
        out_specs=pl.BlockSpec((tm, tn), lambda i, jj: (i, _col(jj, nc))),
        out_shape=jax.ShapeDtypeStruct((m, D_FF), BF16),
        scratch_shapes=[
            pltpu.VMEM((tm, D_MODEL), BF16),
            pltpu.VMEM((CONV_HALO + tm, tn), F32),
            pltpu.VMEM((n_j, CONV_HALO, tn), F32),
        ],
        compiler_params=_params(("arbitrary", "arbitrary")),
        name="ffn_up",
    )(h, g, w_up, w_up, conv)


def _final_norm_kernel(h_ref, g_ref, out_ref):
    out_ref[0] = _rms_rows(h_ref[...], g_ref[...])


def _final_norm(h, g, *, batch, seq, lp, tr):
    skip = META_PAD + N_META
    assert seq % tr == 0
    return pl.pallas_call(
        _final_norm_kernel,
        grid=(batch, seq // tr),
        in_specs=[
            pl.BlockSpec((pl.Element(tr), pl.Element(D_MODEL)), lambda b, c: (pl.multiple_of(b * lp + skip + c * tr, ROW_ALIGN), 0)),
            pl.BlockSpec((1, D_MODEL), lambda b, c: (0, 0)),
        ],
        out_specs=pl.BlockSpec((1, tr, D_MODEL), lambda b, c: (b, c, 0)),
        out_shape=jax.ShapeDtypeStruct((batch, seq, D_MODEL), F32),
        compiler_params=_params(("arbitrary", "arbitrary")),
        name="final_norm",
    )(h, g)


def _decay_tables(chunk):
    log_gamma = jnp.log1p(-jnp.exp2(-5.0 - jnp.arange(RET_HEADS, dtype=F32)))
    idx = jnp.arange(chunk, dtype=F32)
    rel = idx[:, None] - idx[None, :]
    dmask = jnp.where(rel[None] >= 0, jnp.exp(log_gamma[:, None, None] * jnp.maximum(rel, 0.0)[None]), 0.0)
    qd = jnp.exp(log_gamma[:, None] * (idx + 1.0)[None, :])[:, :, None]
    kd = jnp.exp(log_gamma[:, None] * (chunk - 1.0 - idx)[None, :])[:, :, None]
    cd = jnp.exp(log_gamma * chunk)[:, None, None]
    return dmask, qd, kd, cd


def _pool_band(chunk):
    r = jnp.arange(chunk)[:, None] + POOL_HALO
    c = jnp.arange(POOL_HALO + chunk)[None, :]
    lag = r - c
    return jnp.stack([jnp.logical_and(lag >= 0, lag < w) for w in POOL_WINDOWS]).astype(BF16)


def kernel(x, meta_tokens, norm1_g, w_in, pool_w, pool_scale, w_out, norm2_g, w_up, conv_w, conv_b, w_down, final_g):
    batch, seq, d = x.shape
    depth = w_in.shape[0]
    assert d == D_MODEL and meta_tokens.shape == (N_META, D_MODEL)
    assert w_up.shape[2] == 2 * D_FF and w_down.shape[1] == D_FF
    lp = META_PAD + N_META + seq
    m = batch * lp
    tm_norm = m // 4
    tm_out = m // 4
    tm_down = m // 6
    chunk = 384
    assert lp % chunk == 0
    for t in (tm_norm, tm_out, tm_down):
        assert m % t == 0

    meta = jnp.broadcast_to(meta_tokens[None].astype(x.dtype), (batch, N_META, d))
    h = jnp.concatenate([jnp.zeros((batch, META_PAD, d), x.dtype), meta, x], axis=1).reshape(m, d)

    pos = (jnp.arange(lp) - META_PAD).astype(F32)
    inv = jnp.power(ROPE_BASE, -jnp.arange(HALF, dtype=F32) / HALF)
    ang = pos[:, None] * inv[None, :]
    cos = jnp.tile(jnp.cos(ang), (batch, 1))
    sin = jnp.tile(jnp.sin(ang), (batch, 1))
    dmask, qd, kd, cd = _decay_tables(chunk)
    band = _pool_band(chunk)

    for l in range(depth):
        proj = _in_proj(h, norm1_g[l][None], w_in, cos, sin, layer=l, tm=tm_norm, tn=512, nc=6, parts=3)
        mixed = _mixer(proj, dmask, qd, kd, cd, band, pool_w[l].astype(BF16), pool_scale[l][None],
                       batch=batch, lp=lp, chunk=chunk)
        h = _proj_res(mixed, w_out, h, layer=l, tm=tm_out, tn=256, parts=4, name="out_proj")
        act = _ffn_up(h, norm2_g[l][None], w_up, conv_w[l], conv_b[l], layer=l, tm=tm_norm, tn=256, nc=6,
                      parts=(tm_norm // 4,) * 4)
        for kb in range(2):
            h = _proj_res(act, w_down, h, layer=l, tm=tm_down, tn=256, parts=4, name="ffn_down",
                          k_blocks=2, k_block=kb)

    return _final_norm(h, final_g[None], batch=batch, seq=seq, lp=lp, tr=512)
```

```python
import functools

import jax
import jax.numpy as jnp
from jax import lax
from jax.experimental import pallas as pl
from jax.experimental.pallas import tpu as pltpu

D_MODEL = 4096
N_META = 16
RET_W = 2048
POOL_W = 2048
RET_HEADS = 8
HEAD_DIM = 256
HALF = HEAD_DIM // 2
POOL_WINDOWS = (2, 4, 8, 16)
POOL_GROUP = 512
D_FF = 11008
CONV_W = 3
ROPE_BASE = 10000.0
EPS = 1e-6

LANE = 128
BF16_ROWS = 16
ROW_ALIGN = 128
META_PAD = (-N_META) % ROW_ALIGN
POOL_HALO = 128
CONV_HALO = 8
VMEM_LIMIT = 58 * 1024 * 1024

BF16 = jnp.bfloat16
F32 = jnp.float32


def _params(sem):
    return pltpu.CompilerParams(dimension_semantics=sem, vmem_limit_bytes=VMEM_LIMIT)


def _rms_rows(x, g):
    return x * lax.rsqrt(jnp.mean(x * x, axis=-1, keepdims=True) + EPS) * g


def _norm_phase(jj, h_ref, g_ref, hn_ref, nc):
    rc = hn_ref.shape[0] // nc

    @pl.when(jj < nc)
    def _():
        r0 = pl.multiple_of(jj * rc, BF16_ROWS)
        hn_ref[pl.ds(r0, rc), :] = _rms_rows(h_ref[...], g_ref[...]).astype(BF16)


def _norm_specs(tm, nc):
    assert tm % (BF16_ROWS * nc) == 0
    h_spec = pl.BlockSpec((tm // nc, D_MODEL), lambda i, jj: (i * nc + jnp.minimum(jj, nc - 1), 0))
    g_spec = pl.BlockSpec((1, D_MODEL), lambda i, jj: (0, 0))
    return h_spec, g_spec


def _col(jj, nc):
    return jnp.maximum(jj - nc, 0)


def _in_proj_kernel(h_ref, g_ref, w_ref, cos_ref, sin_ref, out_ref, hn_ref, *, nc, tn, n_q, n_rot, parts):
    jj = pl.program_id(1)
    _norm_phase(jj, h_ref, g_ref, hn_ref, nc)

    @pl.when(jj >= nc)
    def _():
        j = jj - nc
        w = w_ref[...].astype(BF16)
        is_rot = j < n_rot
        scale = jnp.where(jnp.logical_and(j >= n_q, is_rot), HEAD_DIM ** -0.5, 1.0).astype(F32)
        tp = hn_ref.shape[0] // parts
        for p in range(parts):
            rows = slice(p * tp, (p + 1) * tp)
            acc = jnp.dot(hn_ref[rows, :], w, preferred_element_type=F32)
            cos = jnp.where(is_rot, cos_ref[rows, :], 1.0) * scale
            sin = jnp.where(is_rot, sin_ref[rows, :], 0.0) * scale
            for hh in range(tn // HEAD_DIM):
                x1 = acc[:, hh * HEAD_DIM: hh * HEAD_DIM + HALF]
                x2 = acc[:, hh * HEAD_DIM + HALF: (hh + 1) * HEAD_DIM]
                out_ref[rows, hh * HEAD_DIM: hh * HEAD_DIM + HALF] = (x1 * cos - x2 * sin).astype(BF16)
                out_ref[rows, hh * HEAD_DIM + HALF: (hh + 1) * HEAD_DIM] = (x1 * sin + x2 * cos).astype(BF16)


def _in_proj(h, g, w, cos, sin, *, layer, tm, tn, nc, parts):
    m = h.shape[0]
    n = w.shape[2]
    assert tm % (BF16_ROWS * parts) == 0
    kern = functools.partial(_in_proj_kernel, nc=nc, tn=tn, n_q=RET_W // tn, n_rot=2 * RET_W // tn, parts=parts)
    h_spec, g_spec = _norm_specs(tm, nc)
    return pl.pallas_call(
        kern,
        grid=(m // tm, nc + n // tn),
        in_specs=[
            h_spec,
            g_spec,
            pl.BlockSpec((None, D_MODEL, tn), lambda i, jj: (layer, 0, _col(jj, nc))),
            pl.BlockSpec((tm, HALF), lambda i, jj: (i, 0)),
            pl.BlockSpec((tm, HALF), lambda i, jj: (i, 0)),
        ],
        out_specs=pl.BlockSpec((tm, tn), lambda i, jj: (i, _col(jj, nc))),
        out_shape=jax.ShapeDtypeStruct((m, n), BF16),
        scratch_shapes=[pltpu.VMEM((tm, D_MODEL), BF16)],
        compiler_params=_params(("arbitrary", "arbitrary")),
        name="in_proj",
    )(h, g, w, cos, sin)


def _mixer_kernel(q_ref, k_ref, v_ref, g_ref, p_ref, dmask_ref, qd_ref, kd_ref, cd_ref, band_ref, pw_ref, ps_ref,
                  out_ref, state_ref, pext_ref, *, chunk):
    c = pl.program_id(1)

    @pl.when(c == 0)
    def _():
        state_ref[...] = jnp.zeros_like(state_ref)
        pext_ref[0:POOL_HALO, :] = jnp.zeros((POOL_HALO, POOL_W), BF16)

    for hh in range(RET_HEADS):
        sl = slice(hh * HEAD_DIM, (hh + 1) * HEAD_DIM)
        q = q_ref[:, sl]
        k = k_ref[:, sl]
        v = v_ref[:, sl]
        scores = lax.dot_general(q, k, (((1,), (1,)), ((), ())), preferred_element_type=F32)
        scores = (scores * dmask_ref[hh]).astype(BF16)
        state = state_ref[hh]
        qdec = (q.astype(F32) * qd_ref[hh]).astype(BF16)
        lhs = jnp.concatenate([scores, qdec], axis=1)
        rhs = jnp.concatenate([v, state.astype(BF16)], axis=0)
        r = jnp.dot(lhs, rhs, preferred_element_type=F32)
        kdec = (k.astype(F32) * kd_ref[hh]).astype(BF16)
        upd = lax.dot_general(kdec, v, (((0,), (0,)), ((), ())), preferred_element_type=F32)
        state_ref[hh] = state * cd_ref[hh] + upd
        r = r * lax.rsqrt(jnp.mean(r * r, axis=-1, keepdims=True) + EPS)
        gate = g_ref[:, sl].astype(F32)
        out_ref[:, sl] = (r * (gate * jax.nn.sigmoid(gate))).astype(BF16)

    pext_ref[POOL_HALO:POOL_HALO + chunk, :] = p_ref[...]
    t = c * chunk - META_PAD + lax.broadcasted_iota(jnp.int32, (chunk, 1), 0)
    valid = t >= 0
    for gi, w in enumerate(POOL_WINDOWS):
        cs = slice(gi * POOL_GROUP, (gi + 1) * POOL_GROUP)
        xg = p_ref[:, cs].astype(F32)
        win = jnp.dot(band_ref[gi], pext_ref[:, cs], preferred_element_type=F32)
        cnt = jnp.clip(t + 1, 1, w).astype(F32)
        mix = jnp.where(valid, win / cnt - xg, 0.0).astype(BF16)
        mg = jnp.dot(mix, pw_ref[gi], preferred_element_type=F32) * ps_ref[:, cs]
        out_ref[:, RET_W + gi * POOL_GROUP: RET_W + (gi + 1) * POOL_GROUP] = mg.astype(BF16)
    pext_ref[0:POOL_HALO, :] = pext_ref[chunk:chunk + POOL_HALO, :]


def _mixer(proj, dmask, qd, kd, cd, band, pool_w, pool_scale, *, batch, lp, chunk):
    nc = lp // chunk
    m = proj.shape[0]
    assert chunk >= POOL_HALO
    kern = functools.partial(_mixer_kernel, chunk=chunk)
    const3 = lambda b, c: (0, 0, 0)
    col_block = lambda n: pl.BlockSpec((chunk, RET_W), lambda b, c: (b * nc + c, n))
    return pl.pallas_call(
        kern,
        grid=(batch, nc),
        in_specs=[
            col_block(0), col_block(1), col_block(2), col_block(3), col_block(4),
            pl.BlockSpec((RET_HEADS, chunk, chunk), const3),
            pl.BlockSpec((RET_HEADS, chunk, 1), const3),
            pl.BlockSpec((RET_HEADS, chunk, 1), const3),
            pl.BlockSpec((RET_HEADS, 1, 1), const3),
            pl.BlockSpec((len(POOL_WINDOWS), chunk, POOL_HALO + chunk), const3),
            pl.BlockSpec((len(POOL_WINDOWS), POOL_GROUP, POOL_GROUP), const3),
            pl.BlockSpec((1, POOL_W), lambda b, c: (0, 0)),
        ],
        out_specs=pl.BlockSpec((chunk, RET_W + POOL_W), lambda b, c: (b * nc + c, 0)),
        out_shape=jax.ShapeDtypeStruct((m, RET_W + POOL_W), BF16),
        scratch_shapes=[
            pltpu.VMEM((RET_HEADS, HEAD_DIM, HEAD_DIM), F32),
            pltpu.VMEM((POOL_HALO + chunk, POOL_W), BF16),
        ],
        compiler_params=_params(("arbitrary", "arbitrary")),
        name="mixer",
    )(proj, proj, proj, proj, proj, dmask, qd, kd, cd, band, pool_w, pool_scale)


def _proj_res_kernel(lhs_ref, w_ref, res_ref, out_ref, *, parts):
    w = w_ref[...].astype(BF16)
    tp = lhs_ref.shape[0] // parts
    for p in range(parts):
        rows = slice(p * tp, (p + 1) * tp)
        out_ref[rows, :] = res_ref[rows, :] + jnp.dot(lhs_ref[rows, :], w, preferred_element_type=F32)


def _proj_res(lhs, w, res, *, layer, tm, tn, parts, name, k_blocks=1, k_block=0):
    m, k = lhs.shape
    n = w.shape[2]
    kb = k // k_blocks
    assert tm % (BF16_ROWS * parts) == 0 and k % k_blocks == 0 and kb % LANE == 0
    return pl.pallas_call(
        functools.partial(_proj_res_kernel, parts=parts),
        grid=(m // tm, n // tn),
        in_specs=[
            pl.BlockSpec((tm, kb), lambda i, j: (i, k_block)),
            pl.BlockSpec((None, kb, tn), lambda i, j: (layer, k_block, j)),
            pl.BlockSpec((tm, tn), lambda i, j: (i, j)),
        ],
        out_specs=pl.BlockSpec((tm, tn), lambda i, j: (i, j)),
        out_shape=jax.ShapeDtypeStruct((m, n), F32),
        compiler_params=_params(("arbitrary", "arbitrary")),
        name=name,
    )(lhs, w, res)


def _ffn_up_kernel(h_ref, g_ref, wa_ref, wb_ref, conv_ref, act_ref, hn_ref, aext_ref, carry_ref,
                   *, nc, tm, parts):
    i = pl.program_id(0)
    jj = pl.program_id(1)
    _norm_phase(jj, h_ref, g_ref, hn_ref, nc)

    @pl.when(jj >= nc)
    def _():
        j = jj - nc
        @pl.when(i == 0)
        def _():
            carry_ref[j] = jnp.zeros(carry_ref.shape[1:], F32)

        wa = wa_ref[...].astype(BF16)
        wb = wb_ref[...].astype(BF16)
        conv = conv_ref[j]
        cb = conv[CONV_W:CONV_W + 1, :]
        aext_ref[0:CONV_HALO, :] = carry_ref[j]
        start = 0
        for tp in parts:
            rows = slice(start, start + tp)
            hn = hn_ref[rows, :]
            a = jnp.dot(hn, wa, preferred_element_type=F32)
            b = jnp.dot(hn, wb, preferred_element_type=F32)
            r0 = CONV_HALO + start
            start += tp
            aext_ref[r0:r0 + tp, :] = a
            ac = cb + a * conv[CONV_W - 1:CONV_W, :]
            for tap in range(CONV_W - 1):
                back = CONV_W - 1 - tap
                ac = ac + aext_ref[r0 - back:r0 - back + tp, :] * conv[tap:tap + 1, :]
            act_ref[rows, :] = (ac * jax.nn.sigmoid(ac) * b).astype(BF16)
        carry_ref[j] = aext_ref[tm:tm + CONV_HALO, :]


def _ffn_up(h, g, w_up, conv_w, conv_b, *, layer, tm, tn, nc, parts):
    m = h.shape[0]
    n_j = D_FF // tn
    assert sum(parts) == tm and all(tp % BF16_ROWS == 0 for tp in parts)
    conv = jnp.concatenate([conv_w, conv_b[None]], axis=0).reshape(CONV_W + 1, n_j, tn).transpose(1, 0, 2)
    kern = functools.partial(_ffn_up_kernel, nc=nc, tm=tm, parts=parts)
    h_spec, g_spec = _norm_specs(tm, nc)
    return pl.pallas_call(
        kern,
        grid=(m // tm, nc + n_j),
        in_specs=[
            h_spec,
            g_spec,
            pl.BlockSpec((None, D_MODEL, tn), lambda i, jj: (layer, 0, _col(jj, nc))),
            pl.BlockSpec((None, D_MODEL, tn), lambda i, jj: (layer, 0, n_j + _col(jj, nc))),
            pl.BlockSpec((n_j, CONV_W + 1, tn), lambda i, jj: (0, 0, 0)),
        ],
        out_specs=pl.BlockSpec((tm, tn), lambda i, jj: (i, _col(jj, nc))),
        out_shape=jax.ShapeDtypeStruct((m, D_FF), BF16),
        scratch_shapes=[
            pltpu.VMEM((tm, D_MODEL), BF16),
            pltpu.VMEM((CONV_HALO + tm, tn), F32),
            pltpu.VMEM((n_j, CONV_HALO, tn), F32),
        ],
        compiler_params=_params(("arbitrary", "arbitrary")),
        name="ffn_up",
    )(h, g, w_up, w_up, conv)


def _final_norm_kernel(h_ref, g_ref, out_ref):
    out_ref[0] = _rms_rows(h_ref[...], g_ref[...])


def _final_norm(h, g, *, batch, seq, lp, tr):
    skip = META_PAD + N_META
    assert seq % tr == 0
    return pl.pallas_call(
        _final_norm_kernel,
        grid=(batch, seq // tr),
        in_specs=[
            pl.BlockSpec((pl.Element(tr), pl.Element(D_MODEL)), lambda b, c: (pl.multiple_of(b * lp + skip + c * tr, ROW_ALIGN), 0)),
            pl.BlockSpec((1, D_MODEL), lambda b, c: (0, 0)),
        ],
        out_specs=pl.BlockSpec((1, tr, D_MODEL), lambda b, c: (b, c, 0)),
        out_shape=jax.ShapeDtypeStruct((batch, seq, D_MODEL), F32),
        compiler_params=_params(("arbitrary", "arbitrary")),
        name="final_norm",
    )(h, g)


def _decay_tables(chunk):
    log_gamma = jnp.log1p(-jnp.exp2(-5.0 - jnp.arange(RET_HEADS, dtype=F32)))
    idx = jnp.arange(chunk, dtype=F32)
    rel = idx[:, None] - idx[None, :]
    dmask = jnp.where(rel[None] >= 0, jnp.exp(log_gamma[:, None, None] * jnp.maximum(rel, 0.0)[None]), 0.0)
    qd = jnp.exp(log_gamma[:, None] * (idx + 1.0)[None, :])[:, :, None]
    kd = jnp.exp(log_gamma[:, None] * (chunk - 1.0 - idx)[None, :])[:, :, None]
    cd = jnp.exp(log_gamma * chunk)[:, None, None]
    return dmask, qd, kd, cd


def _pool_band(chunk):
    r = jnp.arange(chunk)[:, None] + POOL_HALO
    c = jnp.arange(POOL_HALO + chunk)[None, :]
    lag = r - c
    return jnp.stack([jnp.logical_and(lag >= 0, lag < w) for w in POOL_WINDOWS]).astype(BF16)


def kernel(x, meta_tokens, norm1_g, w_in, pool_w, pool_scale, w_out, norm2_g, w_up, conv_w, conv_b, w_down, final_g):
    batch, seq, d = x.shape
    depth = w_in.shape[0]
    assert d == D_MODEL and meta_tokens.shape == (N_META, D_MODEL)
    assert w_up.shape[2] == 2 * D_FF and w_down.shape[1] == D_FF
    lp = META_PAD + N_META + seq
    m = batch * lp
    tm_norm = m // 4
    tm_out = m // 4
    tm_down = m // 6
    chunk = 384
    assert lp % chunk == 0
    for t in (tm_norm, tm_out, tm_down):
        assert m % t == 0

    meta = jnp.broadcast_to(meta_tokens[None].astype(x.dtype), (batch, N_META, d))
    h = jnp.concatenate([jnp.zeros((batch, META_PAD, d), x.dtype), meta, x], axis=1).reshape(m, d)

    pos = (jnp.arange(lp) - META_PAD).astype(F32)
    inv = jnp.power(ROPE_BASE, -jnp.arange(HALF, dtype=F32) / HALF)
    ang = pos[:, None] * inv[None, :]
    cos = jnp.tile(jnp.cos(ang), (batch, 1))
    sin = jnp.tile(jnp.sin(ang), (batch, 1))
    dmask, qd, kd, cd = _decay_tables(chunk)
    band = _pool_band(chunk)

    for l in range(depth):
        proj = _in_proj(h, norm1_g[l][None], w_in, cos, sin, layer=l, tm=tm_norm, tn=512, nc=6, parts=3)
        mixed = _mixer(proj, dmask, qd, kd, cd, band, pool_w[l].astype(BF16), pool_scale[l][None],
                       batch=batch, lp=lp, chunk=chunk)
        h = _proj_res(mixed, w_out, h, layer=l, tm=tm_out, tn=256, parts=4, name="out_proj")
        act = _ffn_up(h, norm2_g[l][None], w_up, conv_w[l], conv_b[l], layer=l, tm=tm_norm, tn=256, nc=6,
                      parts=(tm_norm // 4,) * 4)
        for kb in range(2):
            h = _proj_res(act, w_down, h, layer=l, tm=tm_down, tn=256, parts=4, name="ffn_down",
                          k_blocks=2, k_block=kb)

    return _final_norm(h, final_g[None], batch=batch, seq=seq, lp=lp, tr=512)
```
